```python
import jax, jax.numpy as jnp
from jax import lax
import numpy as np

D_MODEL = 1024
BATCH = 8
SEQ = 4096
DEPTH = 1

CHUNK = 64
RET_HEADS = 4
RET_DK = 128
RET_DV = 128
GDN_HEADS = 4
GDN_DK = 128
GDN_DV = 128
CONV_WIDTH = 4
D_FF = 4 * D_MODEL
ROPE_BASE = 10000.0
NORM_EPS = 1e-6
RET_QK_W = RET_HEADS * RET_DK
RET_W = RET_HEADS * RET_DV
GDN_QK_W = GDN_HEADS * GDN_DK
GDN_W = GDN_HEADS * GDN_DV
MIX_W = RET_W + GDN_W
CONV_CH = 2 * GDN_QK_W + GDN_W
D_IN = 2 * RET_QK_W + 2 * RET_W + 2 * GDN_QK_W + 2 * GDN_W + 2 * GDN_HEADS
N_MOD = 6

kernel_name = "hybrid_retention_gdn_adaln_block"


def rmsnorm(x, w):
    xf = x.astype(jnp.float32)
    y = xf * lax.rsqrt(jnp.mean(xf * xf, axis=-1, keepdims=True) + NORM_EPS)
    return (y * w.astype(jnp.float32)).astype(x.dtype)


def modulate(h, shift, scale):
    return h * (1.0 + scale[:, None, :]) + shift[:, None, :]


def l2norm(x):
    return x * lax.rsqrt(jnp.sum(x * x, axis=-1, keepdims=True) + NORM_EPS)


def rotary(x, positions):
    d = x.shape[-1]
    inv = ROPE_BASE ** (-jnp.arange(0, d, 2, dtype=jnp.float32) / d)
    ang = positions[:, None] * inv[None, :]
    cos = jnp.cos(ang)[None, :, None, :]
    sin = jnp.sin(ang)[None, :, None, :]
    x1, x2 = jnp.split(x, 2, axis=-1)
    return jnp.concatenate([x1 * cos - x2 * sin, x1 * sin + x2 * cos], axis=-1)


def causal_depthwise_conv(x, w):
    ch = x.shape[-1]
    return lax.conv_general_dilated(
        x, w[:, None, :], window_strides=(1,), padding=[(CONV_WIDTH - 1, 0)],
        dimension_numbers=("NWC", "WIO", "NWC"), feature_group_count=ch)


def to_chunks(x):
    b, t, h, d = x.shape
    return x.reshape(b, t // CHUNK, CHUNK, h, d).transpose(0, 3, 1, 2, 4)


def from_chunks(x):
    b, h, n, c, d = x.shape
    return x.transpose(0, 2, 3, 1, 4).reshape(b, n * c, h, d)


def retention_chunkwise(q, k, v):
    n_heads = q.shape[1]
    log_gamma = jnp.log(1.0 - 2.0 ** (-5.0 - jnp.arange(n_heads, dtype=jnp.float32)))
    j = jnp.arange(CHUNK, dtype=jnp.float32)
    intra_decay = jnp.exp(log_gamma[:, None, None] * jnp.abs(j[:, None] - j[None, :]))
    scores = jnp.einsum("bhncd,bhnsd->bhncs", q, k) * intra_decay[None, :, None]
    o_intra = jnp.einsum("bhncs,bhnsv->bhncv", scores, v)
    q_dec = q * jnp.exp(log_gamma[:, None] * (j + 1.0)[None, :])[None, :, None, :, None]
    k_dec = k * jnp.exp(log_gamma[:, None] * (CHUNK - 1.0 - j)[None, :])[None, :, None, :, None]
    chunk_decay = jnp.exp(log_gamma * CHUNK)[None, :, None, None]

    def step(state, xs):
        qd, kd, vc = xs
        o = jnp.einsum("bhcd,bhdv->bhcv", qd, state)
        state = state * chunk_decay + jnp.einsum("bhcd,bhcv->bhdv", kd, vc)
        return state, o

    b, h, _, _, dk = q.shape
    s0 = jnp.zeros((b, h, dk, v.shape[-1]), jnp.float32)
    xs = (jnp.moveaxis(q_dec, 2, 0), jnp.moveaxis(k_dec, 2, 0), jnp.moveaxis(v, 2, 0))
    _, o_cross = lax.scan(step, s0, xs)
    return o_intra + jnp.moveaxis(o_cross, 0, 2)


def gated_delta_rule_chunked(q, k, v, g, beta):
    gc = jnp.cumsum(g, axis=-1)
    idx = jnp.arange(CHUNK)
    tril = idx[:, None] >= idx[None, :]
    strict = idx[:, None] > idx[None, :]
    diff = gc[..., :, None] - gc[..., None, :]
    decay = jnp.exp(jnp.where(tril, diff, -jnp.inf))
    kk = jnp.einsum("bhncd,bhnsd->bhncs", k, k)
    lower = jnp.where(strict, beta[..., :, None] * kk * decay, 0.0)
    eye = jnp.eye(CHUNK, dtype=jnp.float32)
    rhs = jnp.concatenate([beta[..., None] * v, beta[..., None] * k * jnp.exp(gc)[..., None]], axis=-1)
    a_mat = jnp.broadcast_to(eye, lower.shape) + lower
    sol = lax.linalg.triangular_solve(a_mat, rhs, left_side=True, lower=True, unit_diagonal=True)
    dv = v.shape[-1]
    u, w = sol[..., :dv], sol[..., dv:]
    attn = jnp.einsum("bhncd,bhnsd->bhncs", q, k) * decay
    q_g = q * jnp.exp(gc)[..., None]
    g_last = gc[..., -1]
    k_dec = k * jnp.exp(g_last[..., None] - gc)[..., None]

    def step(state, xs):
        qg, at, uc, wc, kd, gl = xs
        v_new = uc - jnp.einsum("bhcd,bhdv->bhcv", wc, state)
        o = jnp.einsum("bhcd,bhdv->bhcv", qg, state) + jnp.einsum("bhcs,bhsv->bhcv", at, v_new)
        state = state * jnp.exp(gl)[:, :, None, None] + jnp.einsum("bhcd,bhcv->bhdv", kd, v_new)
        return state, o

    b, h, _, _, dk = q.shape
    s0 = jnp.zeros((b, h, dk, dv), jnp.float32)
    xs = tuple(jnp.moveaxis(t, 2, 0) for t in (q_g, attn, u, w, k_dec, g_last))
    _, o = lax.scan(step, s0, xs)
    return jnp.moveaxis(o, 0, 2)


def token_mix(proj, positions, conv_w, a_log, dt_bias, ret_norm_w, gdn_norm_w):
    b, t, _ = proj.shape
    sizes = (RET_QK_W, RET_QK_W, RET_W, RET_W, GDN_QK_W, GDN_QK_W, GDN_W, GDN_W, GDN_HEADS, GDN_HEADS)
    cuts = np.cumsum(sizes)[:-1].tolist()
    rq, rk, rv, rg, gq, gk, gv, gz, ga, gb = jnp.split(proj, cuts, axis=-1)

    rq = rotary(rq.reshape(b, t, RET_HEADS, RET_DK), positions)
    rk = rotary(rk.reshape(b, t, RET_HEADS, RET_DK), positions) * (RET_DK ** -0.5)
    rv = rv.reshape(b, t, RET_HEADS, RET_DV)
    ro = from_chunks(retention_chunkwise(to_chunks(rq), to_chunks(rk), to_chunks(rv)))
    mu = jnp.mean(ro, axis=-1, keepdims=True)
    var = jnp.mean(jnp.square(ro - mu), axis=-1, keepdims=True)
    ro = ((ro - mu) * lax.rsqrt(var + NORM_EPS)).reshape(b, t, RET_W) * ret_norm_w
    ret_out = ro * jax.nn.silu(rg)

    qkv = jax.nn.silu(causal_depthwise_conv(jnp.concatenate([gq, gk, gv], axis=-1), conv_w))
    gq, gk, gv = jnp.split(qkv, [GDN_QK_W, 2 * GDN_QK_W], axis=-1)
    gq = l2norm(gq.reshape(b, t, GDN_HEADS, GDN_DK)) * (GDN_DK ** -0.5)
    gk = l2norm(gk.reshape(b, t, GDN_HEADS, GDN_DK))
    gv = gv.reshape(b, t, GDN_HEADS, GDN_DV)
    beta = jax.nn.sigmoid(gb)
    g = -jnp.exp(a_log) * jax.nn.softplus(ga + dt_bias)
    n = t // CHUNK
    g_c = g.reshape(b, n, CHUNK, GDN_HEADS).transpose(0, 3, 1, 2)
    beta_c = beta.reshape(b, n, CHUNK, GDN_HEADS).transpose(0, 3, 1, 2)
    go = from_chunks(gated_delta_rule_chunked(to_chunks(gq), to_chunks(gk), to_chunks(gv), g_c, beta_c))
    go = go * lax.rsqrt(jnp.mean(go * go, axis=-1, keepdims=True) + NORM_EPS) * gdn_norm_w
    gdn_out = go.reshape(b, t, GDN_W) * jax.nn.silu(gz)

    return jnp.concatenate([ret_out, gdn_out], axis=-1)


def setup_inputs(seed: int = 0) -> dict:
    key = jax.random.key(seed)
    ks = jax.random.split(key, 20)
    f32 = jnp.float32
    nrm = lambda k, shape, s: jax.random.normal(k, shape, f32) * s
    dt = jnp.exp(jax.random.uniform(ks[7], (DEPTH, GDN_HEADS), f32, np.log(1e-3), np.log(1e-1)))
    return {
        "x": nrm(ks[0], (BATCH, SEQ, D_MODEL), 1.0),
        "c": nrm(ks[1], (BATCH, D_MODEL), 1.0),
        "ada_w": nrm(ks[2], (DEPTH, D_MODEL, N_MOD * D_MODEL), 0.5 * D_MODEL ** -0.5),
        "ada_b": nrm(ks[3], (DEPTH, N_MOD * D_MODEL), 0.02),
        "norm_mix_w": 1.0 + nrm(ks[4], (DEPTH, D_MODEL), 0.02),
        "w_in": nrm(ks[5], (DEPTH, D_MODEL, D_IN), D_MODEL ** -0.5),
        "conv_w": nrm(ks[6], (DEPTH, CONV_WIDTH, CONV_CH), CONV_WIDTH ** -0.5),
        "a_log": jnp.log(jax.random.uniform(ks[8], (DEPTH, GDN_HEADS), f32, 1.0, 16.0)),
        "dt_bias": dt + jnp.log(-jnp.expm1(-dt)),
        "ret_norm_w": 1.0 + nrm(ks[9], (DEPTH, RET_W), 0.02),
        "gdn_norm_w": 1.0 + nrm(ks[10], (DEPTH, GDN_DV), 0.02),
        "w_out": nrm(ks[11], (DEPTH, MIX_W, D_MODEL), MIX_W ** -0.5),
        "norm_mlp_w": 1.0 + nrm(ks[12], (DEPTH, D_MODEL), 0.02),
        "w_ff1": nrm(ks[13], (DEPTH, D_MODEL, D_FF), D_MODEL ** -0.5),
        "w_ff2": nrm(ks[14], (DEPTH, D_FF, D_MODEL), D_FF ** -0.5),
        "norm_final_w": 1.0 + nrm(ks[15], (D_MODEL,), 0.02),
    }


def reference(x, c, ada_w, ada_b, norm_mix_w, w_in, conv_w, a_log, dt_bias, ret_norm_w,
              gdn_norm_w, w_out, norm_mlp_w, w_ff1, w_ff2, norm_final_w):
    t = x.shape[1]
    positions = jnp.arange(t, dtype=jnp.float32)
    f32 = jnp.float32
    for l in range(DEPTH):
        mod = jax.nn.silu(c) @ ada_w[l] + ada_b[l]
        shift_a, scale_a, gate_a, shift_m, scale_m, gate_m = jnp.split(mod, N_MOD, axis=-1)
        h = modulate(rmsnorm(x, norm_mix_w[l]), shift_a, scale_a)
        proj = (h @ w_in[l]).astype(f32)
        mixed = token_mix(proj, positions, conv_w[l].astype(f32), a_log[l].astype(f32),
                          dt_bias[l].astype(f32), ret_norm_w[l].astype(f32),
                          gdn_norm_w[l].astype(f32)).astype(x.dtype)
        x = x + gate_a[:, None, :] * (mixed @ w_out[l])
        h = modulate(rmsnorm(x, norm_mlp_w[l]), shift_m, scale_m)
        x = x + gate_m[:, None, :] * (jnp.square(jax.nn.relu(h @ w_ff1[l])) @ w_ff2[l])
    return rmsnorm(x, norm_final_w)
```

```python
import functools
import math

import jax
import jax.numpy as jnp
from jax import lax
from jax.experimental import pallas as pl
from jax.experimental.pallas import tpu as pltpu

CHUNK = 64
HEADS = 4
HEAD_DIM = 128
GROUP_W = HEADS * HEAD_DIM
CONV_WIDTH = 4
CONV_CH = 3 * GROUP_W
CONV_PAD = 8
ROPE_BASE = 10000.0
NORM_EPS = 1e-6
N_MOD = 6

TIME_BLOCK = 256
ROW_BLOCK = 512
FF_BLOCK = 1024
ROPE_BLOCK = 512
VMEM_LIMIT_BYTES = 48 * 1024 * 1024

COL_RET = 0
COL_CONV = 4 * GROUP_W
COL_GZ = COL_CONV + CONV_CH
COL_GAB = COL_GZ + GROUP_W
PROJ_W = COL_GAB + 2 * GROUP_W

NT_DIMS = (((1,), (1,)), ((), ()))
TN_DIMS = (((0,), (0,)), ((), ()))

BF16 = jnp.bfloat16
F32 = jnp.float32


def _dot(a, b):
    return jnp.dot(a.astype(BF16), b.astype(BF16), preferred_element_type=F32)


def _dot_nt(a, b):
    return lax.dot_general(a.astype(BF16), b.astype(BF16), NT_DIMS, preferred_element_type=F32)


def _dot_tn(a, b):
    return lax.dot_general(a.astype(BF16), b.astype(BF16), TN_DIMS, preferred_element_type=F32)


def _sigmoid(x):
    return 1.0 / (1.0 + jnp.exp(-x))


def _silu(x):
    return x * _sigmoid(x)


def _softplus(x):
    return jnp.maximum(x, 0.0) + jnp.log1p(jnp.exp(-jnp.abs(x)))


def _rms(x):
    return x * lax.rsqrt(jnp.mean(x * x, axis=-1, keepdims=True) + NORM_EPS)


def _mod_kernel(c_ref, w_ref, b_ref, o_ref):
    o_ref[...] = _dot(_silu(c_ref[...]), w_ref[...]) + b_ref[...]


def _modulation(c, ada_w, ada_b):
    batch, d = c.shape
    n = ada_w.shape[1]
    return pl.pallas_call(
        _mod_kernel,
        grid=(n // d,),
        in_specs=[
            pl.BlockSpec((batch, d), lambda j: (0, 0)),
            pl.BlockSpec((d, d), lambda j: (0, j)),
            pl.BlockSpec((1, d), lambda j: (0, j)),
        ],
        out_specs=pl.BlockSpec((batch, d), lambda j: (0, j)),
        out_shape=jax.ShapeDtypeStruct((batch, n), F32),
        name="modulation",
    )(c, ada_w, ada_b.reshape(1, n))


def _rope_kernel(cos_ref, sin_ref):
    rows = cos_ref.shape[0]
    pos = (pl.program_id(0) * rows + lax.broadcasted_iota(jnp.int32, (rows, HEAD_DIM), 0)).astype(F32)
    lane = lax.broadcasted_iota(jnp.int32, (rows, HEAD_DIM), 1)
    half = HEAD_DIM // 2
    freq = jnp.where(lane < half, lane, lane - half).astype(F32)
    inv = jnp.exp(freq * (-2.0 * math.log(ROPE_BASE) / HEAD_DIM))
    ang = pos * inv
    cos_ref[...] = jnp.cos(ang)
    sin_ref[...] = jnp.where(lane < half, -jnp.sin(ang), jnp.sin(ang))


def _rope_tables(seq):
    spec = pl.BlockSpec((ROPE_BLOCK, HEAD_DIM), lambda i: (i, 0))
    shape = jax.ShapeDtypeStruct((seq, HEAD_DIM), F32)
    return pl.pallas_call(
        _rope_kernel, grid=(seq // ROPE_BLOCK,), in_specs=[], out_specs=[spec, spec],
        out_shape=[shape, shape], name="rope_tables")()


def _mix_kernel(x_ref, mod_ref, nw_ref, w_ref, cos_ref, sin_ref, cw_ref, alog_ref, dtb_ref,
                rnw_ref, gnw_ref, o_ref,
                ret_ref, xpad_ref, gz_ref, gab_ref, rstate_ref, gstate_ref,
                dmask_ref, qdec_ref, kdec_ref):
    tb = x_ref.shape[0]
    n_chunks = tb // CHUNK
    log_gamma = [math.log(1.0 - 2.0 ** (-5.0 - h)) for h in range(HEADS)]

    row = lax.broadcasted_iota(jnp.int32, (tb, tb), 0)
    col = lax.broadcasted_iota(jnp.int32, (tb, tb), 1)

    @pl.when((pl.program_id(0) == 0) & (pl.program_id(1) == 0))
    def _():
        dist = jnp.abs(row - col).astype(F32)
        visible = (col // CHUNK) <= (row // CHUNK)
        ridx = lax.broadcasted_iota(jnp.int32, (tb, HEAD_DIM), 0).astype(F32)
        for h in range(HEADS):
            dmask_ref[h] = jnp.where(visible, jnp.exp(log_gamma[h] * dist), 0.0)
            qdec_ref[h] = jnp.exp(log_gamma[h] * (ridx + 1.0))
            kdec_ref[h] = jnp.exp(log_gamma[h] * (tb - 1.0 - ridx))

    @pl.when(pl.program_id(1) == 0)
    def _():
        rstate_ref[...] = jnp.zeros_like(rstate_ref)
        gstate_ref[...] = jnp.zeros_like(gstate_ref)
        xpad_ref[0:CONV_PAD, :] = jnp.zeros((CONV_PAD, CONV_CH), F32)

    shift = mod_ref[0:1, :]
    scale = mod_ref[1:2, :]
    h_in = (_rms(x_ref[...]) * nw_ref[...] * (1.0 + scale) + shift).astype(BF16)
    ret_ref[...] = jnp.dot(h_in, w_ref[:, COL_RET:COL_CONV], preferred_element_type=F32)
    xpad_ref[CONV_PAD:, :] = jnp.dot(h_in, w_ref[:, COL_CONV:COL_GZ], preferred_element_type=F32)
    gz_ref[...] = jnp.dot(h_in, w_ref[:, COL_GZ:COL_GAB], preferred_element_type=F32)
    gab_ref[...] = jnp.dot(h_in, w_ref[:, COL_GAB:PROJ_W], preferred_element_type=F32)

    cos = cos_ref[...]
    sin = sin_ref[...]
    for h in range(HEADS):
        lanes = slice(h * HEAD_DIM, (h + 1) * HEAD_DIM)
        q = ret_ref[:, h * HEAD_DIM:(h + 1) * HEAD_DIM]
        k = ret_ref[:, GROUP_W + h * HEAD_DIM:GROUP_W + (h + 1) * HEAD_DIM]
        v = ret_ref[:, 2 * GROUP_W + h * HEAD_DIM:2 * GROUP_W + (h + 1) * HEAD_DIM]
        gate = ret_ref[:, 3 * GROUP_W + h * HEAD_DIM:3 * GROUP_W + (h + 1) * HEAD_DIM]
        q = q * cos + pltpu.roll(q, HEAD_DIM // 2, 1) * sin
        k = (k * cos + pltpu.roll(k, HEAD_DIM // 2, 1) * sin) * (HEAD_DIM ** -0.5)
        vb = v.astype(BF16)
        scores = _dot_nt(q, k) * dmask_ref[h]
        state = rstate_ref[h]
        o = _dot(scores, vb) + _dot(q * qdec_ref[h], state)
        rstate_ref[h] = state * math.exp(log_gamma[h] * tb) + _dot_tn(k * kdec_ref[h], vb)
        mu = jnp.mean(o, axis=-1, keepdims=True)
        d = o - mu
        var = jnp.mean(d * d, axis=-1, keepdims=True)
        y = d * lax.rsqrt(var + NORM_EPS) * rnw_ref[:, lanes] * _silu(gate)
        o_ref[:, lanes] = y.astype(o_ref.dtype)

    conv = xpad_ref[CONV_PAD - CONV_WIDTH + 1:CONV_PAD - CONV_WIDTH + 1 + tb, :] * cw_ref[0:1, :]
    for w in range(1, CONV_WIDTH):
        start = CONV_PAD - CONV_WIDTH + 1 + w
        conv = conv + xpad_ref[start:start + tb, :] * cw_ref[w:w + 1, :]
    xpad_ref[0:CONV_PAD, :] = xpad_ref[tb:tb + CONV_PAD, :]
    qkv = _silu(conv)

    g = -jnp.exp(alog_ref[...]) * _softplus(gab_ref[:, 0:GROUP_W] + dtb_ref[...])
    beta = _sigmoid(gab_ref[:, GROUP_W:2 * GROUP_W])
    in_chunk = lax.broadcasted_iota(jnp.int32, (tb, GROUP_W), 0) % CHUNK
    gc = g
    step = 1
    while step < CHUNK:
        gc = gc + jnp.where(in_chunk >= step, pltpu.roll(gc, step, 0), 0.0)
        step *= 2

    same_chunk = (row // CHUNK) == (col // CHUNK)
    causal = same_chunk & (row >= col)
    diag = row == col
    eye = jnp.where(diag, 1.0, 0.0)
    reps = tb // HEAD_DIM

    for h in range(HEADS):
        lanes = slice(h * HEAD_DIM, (h + 1) * HEAD_DIM)
        qh = qkv[:, h * HEAD_DIM:(h + 1) * HEAD_DIM]
        kh = qkv[:, GROUP_W + h * HEAD_DIM:GROUP_W + (h + 1) * HEAD_DIM]
        vh = qkv[:, 2 * GROUP_W + h * HEAD_DIM:2 * GROUP_W + (h + 1) * HEAD_DIM]
        qh = qh * lax.rsqrt(jnp.sum(qh * qh, axis=-1, keepdims=True) + NORM_EPS) * (HEAD_DIM ** -0.5)
        kh = kh * lax.rsqrt(jnp.sum(kh * kh, axis=-1, keepdims=True) + NORM_EPS)
        gch = gc[:, lanes]
        bh = beta[:, lanes]
        g_col = jnp.concatenate([gch] * reps, axis=1)
        b_col = jnp.concatenate([bh] * reps, axis=1)
        g_row = gch.T[0:1, :]
        decay = jnp.exp(jnp.where(causal, g_col - g_row, -jnp.inf))
        kb = kh.astype(BF16)
        kk = _dot_nt(kb, kb)
        qk = _dot_nt(qh, kb)
        power = jnp.where(diag, 0.0, -(b_col * kk * decay))
        inv = eye + power
        for _ in range(5):
            power = _dot(power, power)
            inv = inv + _dot(inv, power)
        e_gc = jnp.exp(gch)
        rhs = jnp.concatenate([bh * vh, bh * kh * e_gc], axis=1)
        uw = _dot(inv, rhs)
        a_uw = _dot(qk * decay, uw)
        q_eff = qh * e_gc - a_uw[:, HEAD_DIM:]
        g_last = jnp.concatenate(
            [jnp.broadcast_to(gch[(c + 1) * CHUNK - 1:(c + 1) * CHUNK, :], (CHUNK, HEAD_DIM))
             for c in range(n_chunks)], axis=0)
        k_dec = kh * jnp.exp(g_last - gch)
        state = gstate_ref[h]
        outs = []
        for c in range(n_chunks):
            rows = slice(c * CHUNK, (c + 1) * CHUNK)
            sb = state.astype(BF16)
            outs.append(_dot(q_eff[rows], sb) + a_uw[rows, :HEAD_DIM])
            k_uw = _dot_tn(k_dec[rows], uw[rows])
            chunk_decay = jnp.exp(gch[(c + 1) * CHUNK - 1:(c + 1) * CHUNK, :])
            state = state * chunk_decay - _dot(k_uw[:, HEAD_DIM:], sb) + k_uw[:, :HEAD_DIM]
        gstate_ref[h] = state
        o = jnp.concatenate(outs, axis=0)
        y = _rms(o) * gnw_ref[...] * _silu(gz_ref[:, lanes])
        o_ref[:, GROUP_W + h * HEAD_DIM:GROUP_W + (h + 1) * HEAD_DIM] = y.astype(o_ref.dtype)


def _token_mix(x, mod, norm_w, w_proj, cos_t, sin_t, conv_w, a_log_rep, dt_rep, ret_norm_w, gdn_norm_w):
    batch, seq, d = x.shape
    tb = TIME_BLOCK
    const = lambda shape: pl.BlockSpec(shape, lambda b, t: (0,) * len(shape))
    return pl.pallas_call(
        _mix_kernel,
        grid=(batch, seq // tb),
        in_specs=[
            pl.BlockSpec((None, tb, d), lambda b, t: (b, t, 0)),
            pl.BlockSpec((None, N_MOD, d), lambda b, t: (b, 0, 0)),
            const((1, d)),
            pl.BlockSpec((d, PROJ_W), lambda b, t: (0, 0), pipeline_mode=pl.Buffered(1)),
            pl.BlockSpec((tb, HEAD_DIM), lambda b, t: (t, 0)),
            pl.BlockSpec((tb, HEAD_DIM), lambda b, t: (t, 0)),
            const((CONV_WIDTH, CONV_CH)),
            const((1, GROUP_W)),
            const((1, GROUP_W)),
            const((1, GROUP_W)),
            const((1, HEAD_DIM)),
        ],
        out_specs=pl.BlockSpec((None, tb, 2 * GROUP_W), lambda b, t: (b, t, 0)),
        out_shape=jax.ShapeDtypeStruct((batch, seq, 2 * GROUP_W), BF16),
        scratch_shapes=[
            pltpu.VMEM((tb, 4 * GROUP_W), F32),
            pltpu.VMEM((tb + CONV_PAD, CONV_CH), F32),
            pltpu.VMEM((tb, GROUP_W), F32),
            pltpu.VMEM((tb, 2 * GROUP_W), F32),
            pltpu.VMEM((HEADS, HEAD_DIM, HEAD_DIM), F32),
            pltpu.VMEM((HEADS, HEAD_DIM, HEAD_DIM), F32),
            pltpu.VMEM((HEADS, tb, tb), F32),
            pltpu.VMEM((HEADS, tb, HEAD_DIM), F32),
            pltpu.VMEM((HEADS, tb, HEAD_DIM), F32),
        ],
        compiler_params=pltpu.CompilerParams(
            dimension_semantics=("arbitrary", "arbitrary"), vmem_limit_bytes=VMEM_LIMIT_BYTES),
        name="token_mix",
    )(x, mod, norm_w, w_proj, cos_t, sin_t, conv_w, a_log_rep, dt_rep, ret_norm_w, gdn_norm_w)


def _channel_kernel(x_ref, mixed_ref, mod_ref, wo_ref, nw_ref, w1_ref, w2_ref, fw_ref, o_ref):
    gate_a = mod_ref[2:3, :]
    shift = mod_ref[3:4, :]
    scale = mod_ref[4:5, :]
    gate_m = mod_ref[5:6, :]
    x1 = x_ref[...] + gate_a * jnp.dot(mixed_ref[...], wo_ref[...], preferred_element_type=F32)
    h = (_rms(x1) * nw_ref[...] * (1.0 + scale) + shift).astype(BF16)
    d_ff = w1_ref.shape[1]
    acc = jnp.zeros(x1.shape, F32)
    for j in range(d_ff // FF_BLOCK):
        cols = slice(j * FF_BLOCK, (j + 1) * FF_BLOCK)
        a = jnp.maximum(jnp.dot(h, w1_ref[:, cols], preferred_element_type=F32), 0.0)
        acc = acc + jnp.dot((a * a).astype(BF16), w2_ref[cols, :], preferred_element_type=F32)
    x2 = x1 + gate_m * acc
    o_ref[...] = _rms(x2) * fw_ref[...]


def _channel_mix(x, mixed, mod, w_out, norm_w, w_ff1, w_ff2, final_w):
    batch, seq, d = x.shape
    d_ff = w_ff1.shape[1]
    rb = ROW_BLOCK
    per_batch = seq // rb
    resident = lambda shape: pl.BlockSpec(shape, lambda i: (0, 0), pipeline_mode=pl.Buffered(1))
    out = pl.pallas_call(
        _channel_kernel,
        grid=(batch * per_batch,),
        in_specs=[
            pl.BlockSpec((rb, d), lambda i: (i, 0)),
            pl.BlockSpec((rb, mixed.shape[-1]), lambda i: (i, 0)),
            pl.BlockSpec((None, N_MOD, d), lambda i: (i // per_batch, 0, 0)),
            resident((mixed.shape[-1], d)),
            pl.BlockSpec((1, d), lambda i: (0, 0)),
            resident((d, d_ff)),
            resident((d_ff, d)),
            pl.BlockSpec((1, d), lambda i: (0, 0)),
        ],
        out_specs=pl.BlockSpec((rb, d), lambda i: (i, 0)),
        out_shape=jax.ShapeDtypeStruct((batch * seq, d), F32),
        compiler_params=pltpu.CompilerParams(
            dimension_semantics=("arbitrary",), vmem_limit_bytes=VMEM_LIMIT_BYTES),
        name="channel_mix",
    )(x.reshape(batch * seq, d), mixed.reshape(batch * seq, -1), mod, w_out, norm_w, w_ff1, w_ff2, final_w)
    return out.reshape(batch, seq, d)


def _widen_in_proj(w_in):
    main = w_in[:, :COL_GAB]
    ga = w_in[:, COL_GAB:COL_GAB + HEADS]
    gb = w_in[:, COL_GAB + HEADS:COL_GAB + 2 * HEADS]
    return jnp.concatenate(
        [main, jnp.repeat(ga, HEAD_DIM, axis=1), jnp.repeat(gb, HEAD_DIM, axis=1)], axis=1).astype(BF16)


def kernel(x, c, ada_w, ada_b, norm_mix_w, w_in, conv_w, a_log, dt_bias, ret_norm_w, gdn_norm_w, w_out,
           norm_mlp_w, w_ff1, w_ff2, norm_final_w):
    batch, seq, d = x.shape
    depth = ada_w.shape[0]
    cos_t, sin_t = _rope_tables(seq)
    for l in range(depth):
        mod = _modulation(c, ada_w[l], ada_b[l]).reshape(batch, N_MOD, d)
        mixed = _token_mix(
            x, mod, norm_mix_w[l].reshape(1, d), _widen_in_proj(w_in[l]), cos_t, sin_t, conv_w[l],
            jnp.repeat(a_log[l], HEAD_DIM).reshape(1, GROUP_W),
            jnp.repeat(dt_bias[l], HEAD_DIM).reshape(1, GROUP_W),
            ret_norm_w[l].reshape(1, GROUP_W), gdn_norm_w[l].reshape(1, HEAD_DIM))
        final_w = norm_final_w if l == depth - 1 else jnp.ones_like(norm_final_w)
        x = _channel_mix(x, mixed, mod, w_out[l].astype(BF16), norm_mlp_w[l].reshape(1, d),
                         w_ff1[l].astype(BF16), w_ff2[l].astype(BF16), final_w.reshape(1, d))
    return x
```

```python
import functools
import math

import jax
import jax.numpy as jnp
from jax import lax
from jax.experimental import pallas as pl
from jax.experimental.pallas import tpu as pltpu

CHUNK = 64
HEADS = 4
HEAD_DIM = 128
GROUP_W = HEADS * HEAD_DIM
CONV_WIDTH = 4
CONV_CH = 3 * GROUP_W
CONV_PAD = 8
ROPE_BASE = 10000.0
NORM_EPS = 1e-6
N_MOD = 6

TIME_BLOCK = 256
ROW_BLOCK = 512
FF_BLOCK = 1024
ROPE_BLOCK = 512
VMEM_LIMIT_BYTES = 48 * 1024 * 1024

COL_RET = 0
COL_CONV = 4 * GROUP_W
COL_GZ = COL_CONV + CONV_CH
COL_GAB = COL_GZ + GROUP_W
PROJ_W = COL_GAB + 2 * GROUP_W

NT_DIMS = (((1,), (1,)), ((), ()))
TN_DIMS = (((0,), (0,)), ((), ()))

BF16 = jnp.bfloat16
F32 = jnp.float32


def _dot(a, b):
    return jnp.dot(a.astype(BF16), b.astype(BF16), preferred_element_type=F32)


def _dot_nt(a, b):
    return lax.dot_general(a.astype(BF16), b.astype(BF16), NT_DIMS, preferred_element_type=F32)


def _dot_tn(a, b):
    return lax.dot_general(a.astype(BF16), b.astype(BF16), TN_DIMS, preferred_element_type=F32)


def _sigmoid(x):
    return 1.0 / (1.0 + jnp.exp(-x))


def _silu(x):
    return x * _sigmoid(x)


def _softplus(x):
    return jnp.maximum(x, 0.0) + jnp.log1p(jnp.exp(-jnp.abs(x)))


def _rms(x):
    return x * lax.rsqrt(jnp.mean(x * x, axis=-1, keepdims=True) + NORM_EPS)


def _mod_kernel(c_ref, w_ref, b_ref, o_ref):
    o_ref[...] = _dot(_silu(c_ref[...]), w_ref[...]) + b_ref[...]


def _modulation(c, ada_w, ada_b):
    batch, d = c.shape
    n = ada_w.shape[1]
    return pl.pallas_call(
        _mod_kernel,
        grid=(n // d,),
        in_specs=[
            pl.BlockSpec((batch, d), lambda j: (0, 0)),
            pl.BlockSpec((d, d), lambda j: (0, j)),
            pl.BlockSpec((1, d), lambda j: (0, j)),
        ],
        out_specs=pl.BlockSpec((batch, d), lambda j: (0, j)),
        out_shape=jax.ShapeDtypeStruct((batch, n), F32),
        name="modulation",
    )(c, ada_w, ada_b.reshape(1, n))


def _rope_kernel(cos_ref, sin_ref):
    rows = cos_ref.shape[0]
    pos = (pl.program_id(0) * rows + lax.broadcasted_iota(jnp.int32, (rows, HEAD_DIM), 0)).astype(F32)
    lane = lax.broadcasted_iota(jnp.int32, (rows, HEAD_DIM), 1)
    half = HEAD_DIM // 2
    freq = jnp.where(lane < half, lane, lane - half).astype(F32)
    inv = jnp.exp(freq * (-2.0 * math.log(ROPE_BASE) / HEAD_DIM))
    ang = pos * inv
    cos_ref[...] = jnp.cos(ang)
    sin_ref[...] = jnp.where(lane < half, -jnp.sin(ang), jnp.sin(ang))


def _rope_tables(seq):
    spec = pl.BlockSpec((ROPE_BLOCK, HEAD_DIM), lambda i: (i, 0))
    shape = jax.ShapeDtypeStruct((seq, HEAD_DIM), F32)
    return pl.pallas_call(
        _rope_kernel, grid=(seq // ROPE_BLOCK,), in_specs=[], out_specs=[spec, spec],
        out_shape=[shape, shape], name="rope_tables")()


def _mix_kernel(x_ref, mod_ref, nw_ref, w_ref, cos_ref, sin_ref, cw_ref, alog_ref, dtb_ref,
                rnw_ref, gnw_ref, o_ref,
                ret_ref, xpad_ref, gz_ref, gab_ref, rstate_ref, gstate_ref,
                dmask_ref, qdec_ref, kdec_ref):
    tb = x_ref.shape[0]
    n_chunks = tb // CHUNK
    log_gamma = [math.log(1.0 - 2.0 ** (-5.0 - h)) for h in range(HEADS)]

    row = lax.broadcasted_iota(jnp.int32, (tb, tb), 0)
    col = lax.broadcasted_iota(jnp.int32, (tb, tb), 1)

    @pl.when((pl.program_id(0) == 0) & (pl.program_id(1) == 0))
    def _():
        dist = jnp.abs(row - col).astype(F32)
        visible = (col // CHUNK) <= (row // CHUNK)
        ridx = lax.broadcasted_iota(jnp.int32, (tb, HEAD_DIM), 0).astype(F32)
        for h in range(HEADS):
            dmask_ref[h] = jnp.where(visible, jnp.exp(log_gamma[h] * dist), 0.0)
            qdec_ref[h] = jnp.exp(log_gamma[h] * (ridx + 1.0))
            kdec_ref[h] = jnp.exp(log_gamma[h] * (tb - 1.0 - ridx))

    @pl.when(pl.program_id(1) == 0)
    def _():
        rstate_ref[...] = jnp.zeros_like(rstate_ref)
        gstate_ref[...] = jnp.zeros_like(gstate_ref)
        xpad_ref[0:CONV_PAD, :] = jnp.zeros((CONV_PAD, CONV_CH), F32)

    shift = mod_ref[0:1, :]
    scale = mod_ref[1:2, :]
    h_in = (_rms(x_ref[...]) * nw_ref[...] * (1.0 + scale) + shift).astype(BF16)
    ret_ref[...] = jnp.dot(h_in, w_ref[:, COL_RET:COL_CONV], preferred_element_type=F32)
    xpad_ref[CONV_PAD:, :] = jnp.dot(h_in, w_ref[:, COL_CONV:COL_GZ], preferred_element_type=F32)
    gz_ref[...] = jnp.dot(h_in, w_ref[:, COL_GZ:COL_GAB], preferred_element_type=F32)
    gab_ref[...] = jnp.dot(h_in, w_ref[:, COL_GAB:PROJ_W], preferred_element_type=F32)

    heads = range(HEADS)

    def head_cols(group, h):
        return slice(group * GROUP_W + h * HEAD_DIM, group * GROUP_W + (h + 1) * HEAD_DIM)

    cos = cos_ref[...]
    sin = sin_ref[...]
    rq, rk, rvb = [], [], []
    for h in heads:
        q = ret_ref[:, head_cols(0, h)]
        k = ret_ref[:, head_cols(1, h)]
        rq.append(q * cos + pltpu.roll(q, HEAD_DIM // 2, 1) * sin)
        rk.append((k * cos + pltpu.roll(k, HEAD_DIM // 2, 1) * sin) * (HEAD_DIM ** -0.5))
        rvb.append(ret_ref[:, head_cols(2, h)].astype(BF16))
    scores = [_dot_nt(rq[h], rk[h]) * dmask_ref[h] for h in heads]
    rstates = [rstate_ref[h] for h in heads]
    ro = [_dot(scores[h], rvb[h]) + _dot(rq[h] * qdec_ref[h], rstates[h]) for h in heads]
    for h in heads:
        rstate_ref[h] = rstates[h] * math.exp(log_gamma[h] * tb) + _dot_tn(rk[h] * kdec_ref[h], rvb[h])
    for h in heads:
        mu = jnp.mean(ro[h], axis=-1, keepdims=True)
        d = ro[h] - mu
        var = jnp.mean(d * d, axis=-1, keepdims=True)
        y = d * lax.rsqrt(var + NORM_EPS) * rnw_ref[:, head_cols(0, h)] * _silu(ret_ref[:, head_cols(3, h)])
        o_ref[:, head_cols(0, h)] = y.astype(o_ref.dtype)

    conv = xpad_ref[CONV_PAD - CONV_WIDTH + 1:CONV_PAD - CONV_WIDTH + 1 + tb, :] * cw_ref[0:1, :]
    for w in range(1, CONV_WIDTH):
        start = CONV_PAD - CONV_WIDTH + 1 + w
        conv = conv + xpad_ref[start:start + tb, :] * cw_ref[w:w + 1, :]
    xpad_ref[0:CONV_PAD, :] = xpad_ref[tb:tb + CONV_PAD, :]
    qkv = _silu(conv)

    g = -jnp.exp(alog_ref[...]) * _softplus(gab_ref[:, 0:GROUP_W] + dtb_ref[...])
    beta = _sigmoid(gab_ref[:, GROUP_W:2 * GROUP_W])
    in_chunk = lax.broadcasted_iota(jnp.int32, (tb, GROUP_W), 0) % CHUNK
    gc = g
    step = 1
    while step < CHUNK:
        gc = gc + jnp.where(in_chunk >= step, pltpu.roll(gc, step, 0), 0.0)
        step *= 2

    gq, gk, gv, gch, bh, e_gc, k_dec = [], [], [], [], [], [], []
    for h in heads:
        qh = qkv[:, head_cols(0, h)]
        kh = qkv[:, head_cols(1, h)]
        gq.append(qh * lax.rsqrt(jnp.sum(qh * qh, axis=-1, keepdims=True) + NORM_EPS) * (HEAD_DIM ** -0.5))
        gk.append(kh * lax.rsqrt(jnp.sum(kh * kh, axis=-1, keepdims=True) + NORM_EPS))
        gv.append(qkv[:, head_cols(2, h)])
        gch.append(gc[:, head_cols(0, h)])
        bh.append(beta[:, head_cols(0, h)])
        e_gc.append(jnp.exp(gch[h]))
        g_last = jnp.concatenate(
            [jnp.broadcast_to(gch[h][(c + 1) * CHUNK - 1:(c + 1) * CHUNK, :], (CHUNK, HEAD_DIM))
             for c in range(n_chunks)], axis=0)
        k_dec.append(gk[h] * jnp.exp(g_last - gch[h]))

    pair = 2 * CHUNK
    units = [(h, p) for p in range(tb // pair) for h in heads]
    prow = lax.broadcasted_iota(jnp.int32, (pair, pair), 0)
    pcol = lax.broadcasted_iota(jnp.int32, (pair, pair), 1)
    causal = ((prow // CHUNK) == (pcol // CHUNK)) & (prow >= pcol)
    diag = prow == pcol
    eye = jnp.where(diag, 1.0, 0.0)

    def unit_rows(p):
        return slice(p * pair, (p + 1) * pair)

    kk = [_dot_nt(gk[h][unit_rows(p)], gk[h][unit_rows(p)]) for h, p in units]
    qk = [_dot_nt(gq[h][unit_rows(p)], gk[h][unit_rows(p)]) for h, p in units]
    decay, power, inv = [], [], []
    for u, (h, p) in enumerate(units):
        gm = gch[h][unit_rows(p)]
        decay.append(jnp.exp(jnp.where(causal, gm - gm.T, -jnp.inf)))
        power.append(jnp.where(diag, 0.0, -(bh[h][unit_rows(p)] * kk[u] * decay[u])))
        inv.append(eye + power[u])
    level = 2
    while level < CHUNK:
        power = [_dot(pw, pw) for pw in power]
        inv = [iv + _dot(iv, pw) for iv, pw in zip(inv, power)]
        level *= 2
    uw = []
    for u, (h, p) in enumerate(units):
        rows = unit_rows(p)
        b = bh[h][rows]
        rhs = jnp.concatenate([b * gv[h][rows], b * gk[h][rows] * e_gc[h][rows]], axis=1)
        uw.append(_dot(inv[u], rhs))
    a_uw = [_dot(qk[u] * decay[u], uw[u]) for u in range(len(units))]
    q_eff, k_uw = [], []
    for u, (h, p) in enumerate(units):
        rows = unit_rows(p)
        q_eff.append(gq[h][rows] * e_gc[h][rows] - a_uw[u][:, HEAD_DIM:])
        kd_t = k_dec[h][rows].T
        k_uw.append([_dot(jnp.where(pcol // CHUNK == c, kd_t, 0.0), uw[u])
                     for c in range(pair // CHUNK)])
    gstates = [gstate_ref[h] for h in heads]
    outs = [[] for _ in heads]
    for c in range(n_chunks):
        p, cl = divmod(c, pair // CHUNK)
        for h in heads:
            u = p * HEADS + h
            local = slice(cl * CHUNK, (cl + 1) * CHUNK)
            sb = gstates[h].astype(BF16)
            outs[h].append(_dot(q_eff[u][local], sb) + a_uw[u][local, :HEAD_DIM])
            chunk_decay = jnp.exp(gch[h][(c + 1) * CHUNK - 1:(c + 1) * CHUNK, :])
            gstates[h] = (gstates[h] * chunk_decay - _dot(k_uw[u][cl][:, HEAD_DIM:], sb)
                          + k_uw[u][cl][:, :HEAD_DIM])
    for h in heads:
        gstate_ref[h] = gstates[h]
        o = jnp.concatenate(outs[h], axis=0)
        y = _rms(o) * gnw_ref[...] * _silu(gz_ref[:, head_cols(0, h)])
        o_ref[:, head_cols(1, h)] = y.astype(o_ref.dtype)


def _token_mix(x, mod, norm_w, w_proj, cos_t, sin_t, conv_w, a_log_rep, dt_rep, ret_norm_w, gdn_norm_w):
    batch, seq, d = x.shape
    tb = TIME_BLOCK
    const = lambda shape: pl.BlockSpec(shape, lambda b, t: (0,) * len(shape))
    return pl.pallas_call(
        _mix_kernel,
        grid=(batch, seq // tb),
        in_specs=[
            pl.BlockSpec((None, tb, d), lambda b, t: (b, t, 0)),
            pl.BlockSpec((None, N_MOD, d), lambda b, t: (b, 0, 0)),
            const((1, d)),
            pl.BlockSpec((d, PROJ_W), lambda b, t: (0, 0), pipeline_mode=pl.Buffered(1)),
            pl.BlockSpec((tb, HEAD_DIM), lambda b, t: (t, 0)),
            pl.BlockSpec((tb, HEAD_DIM), lambda b, t: (t, 0)),
            const((CONV_WIDTH, CONV_CH)),
            const((1, GROUP_W)),
            const((1, GROUP_W)),
            const((1, GROUP_W)),
            const((1, HEAD_DIM)),
        ],
        out_specs=pl.BlockSpec((None, tb, 2 * GROUP_W), lambda b, t: (b, t, 0)),
        out_shape=jax.ShapeDtypeStruct((batch, seq, 2 * GROUP_W), BF16),
        scratch_shapes=[
            pltpu.VMEM((tb, 4 * GROUP_W), F32),
            pltpu.VMEM((tb + CONV_PAD, CONV_CH), F32),
            pltpu.VMEM((tb, GROUP_W), F32),
            pltpu.VMEM((tb, 2 * GROUP_W), F32),
            pltpu.VMEM((HEADS, HEAD_DIM, HEAD_DIM), F32),
            pltpu.VMEM((HEADS, HEAD_DIM, HEAD_DIM), F32),
            pltpu.VMEM((HEADS, tb, tb), F32),
            pltpu.VMEM((HEADS, tb, HEAD_DIM), F32),
            pltpu.VMEM((HEADS, tb, HEAD_DIM), F32),
        ],
        compiler_params=pltpu.CompilerParams(
            dimension_semantics=("arbitrary", "arbitrary"), vmem_limit_bytes=VMEM_LIMIT_BYTES),
        name="token_mix",
    )(x, mod, norm_w, w_proj, cos_t, sin_t, conv_w, a_log_rep, dt_rep, ret_norm_w, gdn_norm_w)


def _channel_kernel(x_ref, mixed_ref, mod_ref, wo_ref, nw_ref, w1_ref, w2_ref, fw_ref, o_ref):
    gate_a = mod_ref[2:3, :]
    shift = mod_ref[3:4, :]
    scale = mod_ref[4:5, :]
    gate_m = mod_ref[5:6, :]
    x1 = x_ref[...] + gate_a * jnp.dot(mixed_ref[...], wo_ref[...], preferred_element_type=F32)
    h = (_rms(x1) * nw_ref[...] * (1.0 + scale) + shift).astype(BF16)
    d_ff = w1_ref.shape[1]
    acc = jnp.zeros(x1.shape, F32)
    for j in range(d_ff // FF_BLOCK):
        cols = slice(j * FF_BLOCK, (j + 1) * FF_BLOCK)
        a = jnp.maximum(jnp.dot(h, w1_ref[:, cols], preferred_element_type=F32), 0.0)
        acc = acc + jnp.dot((a * a).astype(BF16), w2_ref[cols, :], preferred_element_type=F32)
    x2 = x1 + gate_m * acc
    o_ref[...] = _rms(x2) * fw_ref[...]


def _channel_mix(x, mixed, mod, w_out, norm_w, w_ff1, w_ff2, final_w):
    batch, seq, d = x.shape
    d_ff = w_ff1.shape[1]
    rb = ROW_BLOCK
    per_batch = seq // rb
    resident = lambda shape: pl.BlockSpec(shape, lambda i: (0, 0), pipeline_mode=pl.Buffered(1))
    out = pl.pallas_call(
        _channel_kernel,
        grid=(batch * per_batch,),
        in_specs=[
            pl.BlockSpec((rb, d), lambda i: (i, 0)),
            pl.BlockSpec((rb, mixed.shape[-1]), lambda i: (i, 0)),
            pl.BlockSpec((None, N_MOD, d), lambda i: (i // per_batch, 0, 0)),
            resident((mixed.shape[-1], d)),
            pl.BlockSpec((1, d), lambda i: (0, 0)),
            resident((d, d_ff)),
            resident((d_ff, d)),
            pl.BlockSpec((1, d), lambda i: (0, 0)),
        ],
        out_specs=pl.BlockSpec((rb, d), lambda i: (i, 0)),
        out_shape=jax.ShapeDtypeStruct((batch * seq, d), F32),
        compiler_params=pltpu.CompilerParams(
            dimension_semantics=("arbitrary",), vmem_limit_bytes=VMEM_LIMIT_BYTES),
        name="channel_mix",
    )(x.reshape(batch * seq, d), mixed.reshape(batch * seq, -1), mod, w_out, norm_w, w_ff1, w_ff2, final_w)
    return out.reshape(batch, seq, d)


def _widen_in_proj(w_in):
    main = w_in[:, :COL_GAB]
    ga = w_in[:, COL_GAB:COL_GAB + HEADS]
    gb = w_in[:, COL_GAB + HEADS:COL_GAB + 2 * HEADS]
    return jnp.concatenate(
        [main, jnp.repeat(ga, HEAD_DIM, axis=1), jnp.repeat(gb, HEAD_DIM, axis=1)], axis=1).astype(BF16)


def kernel(x, c, ada_w, ada_b, norm_mix_w, w_in, conv_w, a_log, dt_bias, ret_norm_w, gdn_norm_w, w_out,
           norm_mlp_w, w_ff1, w_ff2, norm_final_w):
    batch, seq, d = x.shape
    depth = ada_w.shape[0]
    cos_t, sin_t = _rope_tables(seq)
    for l in range(depth):
        mod = _modulation(c, ada_w[l], ada_b[l]).reshape(batch, N_MOD, d)
        mixed = _token_mix(
            x, mod, norm_mix_w[l].reshape(1, d), _widen_in_proj(w_in[l]), cos_t, sin_t, conv_w[l],
            jnp.repeat(a_log[l], HEAD_DIM).reshape(1, GROUP_W),
            jnp.repeat(dt_bias[l], HEAD_DIM).reshape(1, GROUP_W),
            ret_norm_w[l].reshape(1, GROUP_W), gdn_norm_w[l].reshape(1, HEAD_DIM))
        final_w = norm_final_w if l == depth - 1 else jnp.ones_like(norm_final_w)
        x = _channel_mix(x, mixed, mod, w_out[l].astype(BF16), norm_mlp_w[l].reshape(1, d),
                         w_ff1[l].astype(BF16), w_ff2[l].astype(BF16), final_w.reshape(1, d))
    return x
```

```python
import math

import jax
import jax.numpy as jnp
from jax import lax
from jax.experimental import pallas as pl
from jax.experimental.pallas import tpu as pltpu

CHUNK = 64
HEADS = 4
HEAD_DIM = 128
GROUP_W = HEADS * HEAD_DIM
CONV_WIDTH = 4
CONV_CH = 3 * GROUP_W
CONV_PAD = 8
ROPE_BASE = 10000.0
NORM_EPS = 1e-6
N_MOD = 6

TIME_BLOCK = 256
PROJ_TILE = 256
ROW_BLOCK = 512
FF_BLOCK = 1024
ROPE_BLOCK = 512
VMEM_LIMIT_BYTES = 48 * 1024 * 1024

COL_RET = 0
COL_CONV = 4 * GROUP_W
COL_GZ = COL_CONV + CONV_CH
COL_GAB = COL_GZ + GROUP_W
GAB_W = 128
PROJ_W = COL_GAB + GAB_W

NT_DIMS = (((1,), (1,)), ((), ()))
TN_DIMS = (((0,), (0,)), ((), ()))

BF16 = jnp.bfloat16
F32 = jnp.float32


def _dot(a, b):
    return jnp.dot(a.astype(BF16), b.astype(BF16), preferred_element_type=F32)


def _dot_nt(a, b):
    return lax.dot_general(a.astype(BF16), b.astype(BF16), NT_DIMS, preferred_element_type=F32)


def _dot_tn(a, b):
    return lax.dot_general(a.astype(BF16), b.astype(BF16), TN_DIMS, preferred_element_type=F32)


def _pack_rows(w):
    k, n = w.shape
    return lax.bitcast_convert_type(w.astype(BF16).reshape(k // 2, 2, n).transpose(0, 2, 1), jnp.uint32)


def _unpack_rows(w):
    return pltpu.bitcast(w, BF16)


def _sigmoid(x):
    return 1.0 / (1.0 + jnp.exp(-x))


def _silu(x):
    return x * _sigmoid(x)


def _softplus(x):
    return jnp.maximum(x, 0.0) + jnp.log(1.0 + jnp.exp(-jnp.abs(x)))


def _rms(x):
    return x * lax.rsqrt(jnp.mean(x * x, axis=-1, keepdims=True) + NORM_EPS)


def _mod_kernel(c_ref, w_ref, b_ref, o_ref):
    o_ref[...] = _dot(_silu(c_ref[...]), w_ref[...]) + b_ref[...]


def _modulation(c, ada_w, ada_b):
    batch, d = c.shape
    n = ada_w.shape[1]
    return pl.pallas_call(
        _mod_kernel,
        grid=(n // d,),
        in_specs=[
            pl.BlockSpec((batch, d), lambda j: (0, 0)),
            pl.BlockSpec((d, d), lambda j: (0, j)),
            pl.BlockSpec((1, d), lambda j: (0, j)),
        ],
        out_specs=pl.BlockSpec((batch, d), lambda j: (0, j)),
        out_shape=jax.ShapeDtypeStruct((batch, n), F32),
        name="modulation",
    )(c, ada_w, ada_b.reshape(1, n))


def _rope_kernel(cos_ref, sin_ref):
    rows = cos_ref.shape[0]
    pos = (pl.program_id(0) * rows + lax.broadcasted_iota(jnp.int32, (rows, HEAD_DIM), 0)).astype(F32)
    lane = lax.broadcasted_iota(jnp.int32, (rows, HEAD_DIM), 1)
    half = HEAD_DIM // 2
    freq = jnp.where(lane < half, lane, lane - half).astype(F32)
    inv = jnp.exp(freq * (-2.0 * math.log(ROPE_BASE) / HEAD_DIM))
    ang = pos * inv
    cos_ref[...] = jnp.cos(ang)
    sin_ref[...] = jnp.where(lane < half, -jnp.sin(ang), jnp.sin(ang))


def _rope_tables(seq):
    spec = pl.BlockSpec((ROPE_BLOCK, HEAD_DIM), lambda i: (i, 0))
    shape = jax.ShapeDtypeStruct((seq, HEAD_DIM), F32)
    return pl.pallas_call(
        _rope_kernel, grid=(seq // ROPE_BLOCK,), in_specs=[], out_specs=[spec, spec],
        out_shape=[shape, shape], name="rope_tables")()


def _mix_kernel(x_ref, mod_ref, nw_ref, w_ref, cos_ref, sin_ref, cw_ref, alog_ref, dtb_ref,
                rnw_ref, gnw_ref, o_ref,
                ret_ref, xpad_ref, gz_ref, gab_ref, rstate_ref, gstate_ref,
                dmask_ref, qdec_ref, kdec_ref):
    tb = x_ref.shape[0]
    n_chunks = tb // CHUNK
    log_gamma = [math.log(1.0 - 2.0 ** (-5.0 - h)) for h in range(HEADS)]
    key_scale = HEAD_DIM ** -0.5

    @pl.when((pl.program_id(0) == 0) & (pl.program_id(1) == 0))
    def _():
        row = lax.broadcasted_iota(jnp.int32, (tb, tb), 0)
        col = lax.broadcasted_iota(jnp.int32, (tb, tb), 1)
        dist = jnp.abs(row - col).astype(F32)
        visible = (col // CHUNK) <= (row // CHUNK)
        ridx = lax.broadcasted_iota(jnp.int32, (tb, HEAD_DIM), 0).astype(F32)
        for h in range(HEADS):
            dmask_ref[h] = jnp.where(visible, jnp.exp(log_gamma[h] * dist) * key_scale, 0.0)
            qdec_ref[h] = jnp.exp(log_gamma[h] * (ridx + 1.0))
            kdec_ref[h] = jnp.exp(log_gamma[h] * (tb - 1.0 - ridx)) * key_scale

    @pl.when(pl.program_id(1) == 0)
    def _():
        rstate_ref[...] = jnp.zeros_like(rstate_ref)
        gstate_ref[...] = jnp.zeros_like(gstate_ref)
        xpad_ref[0:CONV_PAD, :] = jnp.zeros((CONV_PAD, CONV_CH), F32)

    shift = mod_ref[0:1, :]
    scale = mod_ref[1:2, :]
    h_in = (_rms(x_ref[...]) * (nw_ref[...] * (1.0 + scale)) + shift).astype(BF16)

    def project(dest_ref, row0, col0, width):
        for c in range(0, width, PROJ_TILE):
            n = min(PROJ_TILE, width - c)
            w_tile = _unpack_rows(w_ref[:, col0 + c:col0 + c + n])
            dest_ref[row0:row0 + tb, c:c + n] = jnp.dot(h_in, w_tile, preferred_element_type=F32)

    project(ret_ref, 0, COL_RET, COL_CONV - COL_RET)
    project(xpad_ref, CONV_PAD, COL_CONV, CONV_CH)
    project(gz_ref, 0, COL_GZ, GROUP_W)
    project(gab_ref, 0, COL_GAB, GAB_W)

    heads = range(HEADS)

    def head_cols(group, h):
        return slice(group * GROUP_W + h * HEAD_DIM, group * GROUP_W + (h + 1) * HEAD_DIM)

    cos = cos_ref[...]
    sin = sin_ref[...]
    rq, rk, rvb = [], [], []
    for h in heads:
        q = ret_ref[:, head_cols(0, h)]
        k = ret_ref[:, head_cols(1, h)]
        rq.append(q * cos + pltpu.roll(q, HEAD_DIM // 2, 1) * sin)
        rk.append(k * cos + pltpu.roll(k, HEAD_DIM // 2, 1) * sin)
        rvb.append(ret_ref[:, head_cols(2, h)].astype(BF16))
    scores = [_dot_nt(rq[h], rk[h]) * dmask_ref[h] for h in heads]
    rstates = [rstate_ref[h] for h in heads]
    ro = [_dot(scores[h], rvb[h]) + _dot(rq[h] * qdec_ref[h], rstates[h]) for h in heads]
    for h in heads:
        rstate_ref[h] = rstates[h] * math.exp(log_gamma[h] * tb) + _dot_tn(rk[h] * kdec_ref[h], rvb[h])
    for h in heads:
        mu = jnp.mean(ro[h], axis=-1, keepdims=True)
        d = ro[h] - mu
        var = jnp.mean(d * d, axis=-1, keepdims=True)
        y = d * lax.rsqrt(var + NORM_EPS) * rnw_ref[:, head_cols(0, h)] * _silu(ret_ref[:, head_cols(3, h)])
        o_ref[:, head_cols(0, h)] = y.astype(o_ref.dtype)

    conv = xpad_ref[CONV_PAD - CONV_WIDTH + 1:CONV_PAD - CONV_WIDTH + 1 + tb, :] * cw_ref[0:1, :]
    for w in range(1, CONV_WIDTH):
        start = CONV_PAD - CONV_WIDTH + 1 + w
        conv = conv + xpad_ref[start:start + tb, :] * cw_ref[w:w + 1, :]
    xpad_ref[0:CONV_PAD, :] = xpad_ref[tb:tb + CONV_PAD, :]
    qkv = _silu(conv)

    gab = gab_ref[...]
    g = -jnp.exp(alog_ref[...]) * _softplus(gab + dtb_ref[...])
    beta = _sigmoid(gab)
    in_chunk = lax.broadcasted_iota(jnp.int32, (tb, GAB_W), 0) % CHUNK
    gc = g
    step = 1
    while step < CHUNK:
        gc = gc + jnp.where(in_chunk >= step, pltpu.roll(gc, step, 0), 0.0)
        step *= 2

    gq, gk, gk_b, gv, gch, bh, e_gc, k_dec = [], [], [], [], [], [], [], []
    for h in heads:
        qh = qkv[:, head_cols(0, h)]
        kh = qkv[:, head_cols(1, h)]
        gq.append(qh * lax.rsqrt(jnp.sum(qh * qh, axis=-1, keepdims=True) + NORM_EPS) * (HEAD_DIM ** -0.5))
        gk.append(kh * lax.rsqrt(jnp.sum(kh * kh, axis=-1, keepdims=True) + NORM_EPS))
        gk_b.append(gk[h].astype(BF16))
        gv.append(qkv[:, head_cols(2, h)])
        gch.append(jnp.broadcast_to(gc[:, h:h + 1], (tb, HEAD_DIM)))
        bh.append(jnp.broadcast_to(beta[:, HEADS + h:HEADS + h + 1], (tb, HEAD_DIM)))
        e_gc.append(jnp.exp(gch[h]))
        g_last = jnp.concatenate(
            [jnp.broadcast_to(gch[h][(c + 1) * CHUNK - 1:(c + 1) * CHUNK, :], (CHUNK, HEAD_DIM))
             for c in range(n_chunks)], axis=0)
        k_dec.append(gk[h] * jnp.exp(g_last - gch[h]))

    pair = 2 * CHUNK
    units = [(h, p) for p in range(tb // pair) for h in heads]
    prow = lax.broadcasted_iota(jnp.int32, (pair, pair), 0)
    pcol = lax.broadcasted_iota(jnp.int32, (pair, pair), 1)
    causal = ((prow // CHUNK) == (pcol // CHUNK)) & (prow >= pcol)
    diag = prow == pcol
    eye = jnp.where(diag, 1.0, 0.0)

    def unit_rows(p):
        return slice(p * pair, (p + 1) * pair)

    kk = [_dot_nt(gk_b[h][unit_rows(p)], gk_b[h][unit_rows(p)]) for h, p in units]
    qk = [_dot_nt(gq[h][unit_rows(p)], gk_b[h][unit_rows(p)]) for h, p in units]
    decay, power, inv = [], [], []
    for u, (h, p) in enumerate(units):
        gm = gch[h][unit_rows(p)]
        decay.append(jnp.exp(jnp.where(causal, gm - gm.T, -jnp.inf)))
        power.append(jnp.where(diag, 0.0, -(bh[h][unit_rows(p)] * kk[u] * decay[u])))
        inv.append(eye + power[u])
    level = 2
    while level < CHUNK:
        power = [_dot(pw, pw) for pw in power]
        inv = [iv + _dot(iv, pw) for iv, pw in zip(inv, power)]
        level *= 2
    uw = []
    for u, (h, p) in enumerate(units):
        rows = unit_rows(p)
        b = bh[h][rows]
        rhs = jnp.concatenate([b * gv[h][rows], b * gk[h][rows] * e_gc[h][rows]], axis=1)
        uw.append(_dot(inv[u], rhs))
    a_uw = [_dot(qk[u] * decay[u], uw[u]) for u in range(len(units))]
    q_eff, k_uw = [], []
    for u, (h, p) in enumerate(units):
        rows = unit_rows(p)
        q_eff.append(gq[h][rows] * e_gc[h][rows] - a_uw[u][:, HEAD_DIM:])
        kd_t = k_dec[h][rows].T
        k_uw.append([_dot(jnp.where(pcol // CHUNK == c, kd_t, 0.0), uw[u])
                     for c in range(pair // CHUNK)])
    gstates = [gstate_ref[h] for h in heads]
    outs = [[] for _ in heads]
    for c in range(n_chunks):
        p, cl = divmod(c, pair // CHUNK)
        for h in heads:
            u = p * HEADS + h
            local = slice(cl * CHUNK, (cl + 1) * CHUNK)
            sb = gstates[h].astype(BF16)
            outs[h].append(_dot(q_eff[u][local], sb) + a_uw[u][local, :HEAD_DIM])
            chunk_decay = jnp.exp(gch[h][(c + 1) * CHUNK - 1:(c + 1) * CHUNK, :])
            gstates[h] = (gstates[h] * chunk_decay - _dot(k_uw[u][cl][:, HEAD_DIM:], sb)
                          + k_uw[u][cl][:, :HEAD_DIM])
    for h in heads:
        gstate_ref[h] = gstates[h]
        o = jnp.concatenate(outs[h], axis=0)
        y = _rms(o) * gnw_ref[...] * _silu(gz_ref[:, head_cols(0, h)])
        o_ref[:, head_cols(1, h)] = y.astype(o_ref.dtype)


def _token_mix(x, mod, norm_w, w_proj, cos_t, sin_t, conv_w, a_log_pad, dt_pad, ret_norm_w, gdn_norm_w):
    batch, seq, d = x.shape
    tb = TIME_BLOCK
    const = lambda shape: pl.BlockSpec(shape, lambda b, t: (0,) * len(shape))
    return pl.pallas_call(
        _mix_kernel,
        grid=(batch, seq // tb),
        in_specs=[
            pl.BlockSpec((None, tb, d), lambda b, t: (b, t, 0)),
            pl.BlockSpec((None, N_MOD, d), lambda b, t: (b, 0, 0)),
            const((1, d)),
            pl.BlockSpec((d // 2, PROJ_W), lambda b, t: (0, 0), pipeline_mode=pl.Buffered(1)),
            pl.BlockSpec((tb, HEAD_DIM), lambda b, t: (t, 0)),
            pl.BlockSpec((tb, HEAD_DIM), lambda b, t: (t, 0)),
            const((CONV_WIDTH, CONV_CH)),
            const((1, GAB_W)),
            const((1, GAB_W)),
            const((1, GROUP_W)),
            const((1, HEAD_DIM)),
        ],
        out_specs=pl.BlockSpec((None, tb, 2 * GROUP_W), lambda b, t: (b, t, 0)),
        out_shape=jax.ShapeDtypeStruct((batch, seq, 2 * GROUP_W), BF16),
        scratch_shapes=[
            pltpu.VMEM((tb, 4 * GROUP_W), F32),
            pltpu.VMEM((tb + CONV_PAD, CONV_CH), F32),
            pltpu.VMEM((tb, GROUP_W), F32),
            pltpu.VMEM((tb, GAB_W), F32),
            pltpu.VMEM((HEADS, HEAD_DIM, HEAD_DIM), F32),
            pltpu.VMEM((HEADS, HEAD_DIM, HEAD_DIM), F32),
            pltpu.VMEM((HEADS, tb, tb), F32),
            pltpu.VMEM((HEADS, tb, HEAD_DIM), F32),
            pltpu.VMEM((HEADS, tb, HEAD_DIM), F32),
        ],
        compiler_params=pltpu.CompilerParams(
            dimension_semantics=("arbitrary", "arbitrary"), vmem_limit_bytes=VMEM_LIMIT_BYTES),
        name="token_mix",
    )(x, mod, norm_w, w_proj, cos_t, sin_t, conv_w, a_log_pad, dt_pad, ret_norm_w, gdn_norm_w)


def _channel_kernel(x_ref, mixed_ref, mod_ref, wo_ref, nw_ref, w1_ref, w2_ref, fw_ref, o_ref):
    gate_a = mod_ref[2:3, :]
    shift = mod_ref[3:4, :]
    scale = mod_ref[4:5, :]
    gate_m = mod_ref[5:6, :]
    x1 = x_ref[...] + gate_a * jnp.dot(mixed_ref[...], _unpack_rows(wo_ref[...]), preferred_element_type=F32)
    h = (_rms(x1) * nw_ref[...] * (1.0 + scale) + shift).astype(BF16)
    d_ff = w1_ref.shape[1]
    acc = jnp.zeros(x1.shape, F32)
    for j in range(d_ff // FF_BLOCK):
        cols = slice(j * FF_BLOCK, (j + 1) * FF_BLOCK)
        a = jnp.maximum(jnp.dot(h, _unpack_rows(w1_ref[:, cols]), preferred_element_type=F32), 0.0)
        w2_rows = _unpack_rows(w2_ref[j * FF_BLOCK // 2:(j + 1) * FF_BLOCK // 2, :])
        acc = acc + jnp.dot((a * a).astype(BF16), w2_rows, preferred_element_type=F32)
    x2 = x1 + gate_m * acc
    o_ref[...] = _rms(x2) * fw_ref[...]


def _channel_mix(x, mixed, mod, w_out, norm_w, w_ff1, w_ff2, final_w):
    batch, seq, d = x.shape
    d_ff = w_ff1.shape[1]
    rb = ROW_BLOCK
    per_batch = seq // rb
    resident = lambda shape: pl.BlockSpec(shape, lambda i: (0, 0), pipeline_mode=pl.Buffered(1))
    out = pl.pallas_call(
        _channel_kernel,
        grid=(batch * per_batch,),
        in_specs=[
            pl.BlockSpec((rb, d), lambda i: (i, 0)),
            pl.BlockSpec((rb, mixed.shape[-1]), lambda i: (i, 0)),
            pl.BlockSpec((None, N_MOD, d), lambda i: (i // per_batch, 0, 0)),
            resident((mixed.shape[-1] // 2, d)),
            pl.BlockSpec((1, d), lambda i: (0, 0)),
            resident((d // 2, d_ff)),
            resident((d_ff // 2, d)),
            pl.BlockSpec((1, d), lambda i: (0, 0)),
        ],
        out_specs=pl.BlockSpec((rb, d), lambda i: (i, 0)),
        out_shape=jax.ShapeDtypeStruct((batch * seq, d), F32),
        compiler_params=pltpu.CompilerParams(
            dimension_semantics=("arbitrary",), vmem_limit_bytes=VMEM_LIMIT_BYTES),
        name="channel_mix",
    )(x.reshape(batch * seq, d), mixed.reshape(batch * seq, -1), mod, w_out, norm_w, w_ff1, w_ff2, final_w)
    return out.reshape(batch, seq, d)


def _pad_lanes(v, width):
    return jnp.pad(v, (0, width - v.shape[0])).reshape(1, width)


def kernel(x, c, ada_w, ada_b, norm_mix_w, w_in, conv_w, a_log, dt_bias, ret_norm_w, gdn_norm_w, w_out,
           norm_mlp_w, w_ff1, w_ff2, norm_final_w):
    batch, seq, d = x.shape
    assert ada_w.shape[0] == 1, "single-layer block: the final rmsnorm is fused into the channel-mix call"
    cos_t, sin_t = _rope_tables(seq)
    mod = _modulation(c, ada_w[0], ada_b[0]).reshape(batch, N_MOD, d)
    w_proj = jnp.pad(w_in[0], ((0, 0), (0, PROJ_W - w_in.shape[2])))
    mixed = _token_mix(
        x, mod, norm_mix_w[0].reshape(1, d), _pack_rows(w_proj), cos_t, sin_t, conv_w[0],
        _pad_lanes(a_log[0], GAB_W), _pad_lanes(dt_bias[0], GAB_W),
        ret_norm_w[0].reshape(1, GROUP_W), gdn_norm_w[0].reshape(1, HEAD_DIM))
    return _channel_mix(x, mixed, mod, _pack_rows(w_out[0]), norm_mlp_w[0].reshape(1, d),
                        _pack_rows(w_ff1[0]), _pack_rows(w_ff2[0]), norm_final_w.reshape(1, d))
```

```python
import math

import jax
import jax.numpy as jnp
from jax import lax
from jax.experimental import pallas as pl
from jax.experimental.pallas import tpu as pltpu

CHUNK = 64
HEADS = 4
HEAD_DIM = 128
GROUP_W = HEADS * HEAD_DIM
CONV_WIDTH = 4
CONV_CH = 3 * GROUP_W
CONV_PAD = 8
ROPE_BASE = 10000.0
NORM_EPS = 1e-6
N_MOD = 6

TIME_BLOCK = 256
PROJ_TILE = 256
ROW_BLOCK = 512
FF_BLOCK = 1024
ROPE_BLOCK = 512
LANES = 128
PACK_TILE = 256
VMEM_LIMIT_BYTES = 48 * 1024 * 1024

COL_RET = 0
COL_CONV = 4 * GROUP_W
COL_GZ = COL_CONV + CONV_CH
COL_GAB = COL_GZ + GROUP_W
GAB_W = LANES
PROJ_W = COL_GAB + GAB_W

NT_DIMS = (((1,), (1,)), ((), ()))
TN_DIMS = (((0,), (0,)), ((), ()))

BF16 = jnp.bfloat16
F32 = jnp.float32


def _dot(a, b):
    return jnp.dot(a.astype(BF16), b.astype(BF16), preferred_element_type=F32)


def _dot_nt(a, b):
    return lax.dot_general(a.astype(BF16), b.astype(BF16), NT_DIMS, preferred_element_type=F32)


def _dot_tn(a, b):
    return lax.dot_general(a.astype(BF16), b.astype(BF16), TN_DIMS, preferred_element_type=F32)


def _pack_kernel(n_valid, w_ref, o_ref):
    w = w_ref[...]
    if n_valid % w.shape[1]:
        col = pl.program_id(0) * w.shape[1] + lax.broadcasted_iota(jnp.int32, w.shape, 1)
        w = jnp.where(col < n_valid, w, 0.0)
    o_ref[...] = pltpu.bitcast(w.astype(BF16), jnp.uint32)


def _pack_rows(w, n_out=None):
    k, n = w.shape
    n_out = n if n_out is None else n_out
    tile = PACK_TILE if n_out % PACK_TILE == 0 else LANES
    return pl.pallas_call(
        lambda w_ref, o_ref: _pack_kernel(n, w_ref, o_ref),
        grid=(n_out // tile,),
        in_specs=[pl.BlockSpec((k, tile), lambda j: (0, j))],
        out_specs=pl.BlockSpec((k // 2, tile), lambda j: (0, j)),
        out_shape=jax.ShapeDtypeStruct((k // 2, n_out), jnp.uint32),
        name="pack_weight",
    )(w)


def _unpack_rows(w):
    return pltpu.bitcast(w, BF16)


def _sigmoid(x):
    return 1.0 / (1.0 + jnp.exp(-x))


def _silu(x):
    return x * _sigmoid(x)


def _softplus(x):
    return jnp.maximum(x, 0.0) + jnp.log(1.0 + jnp.exp(-jnp.abs(x)))


def _rms(x):
    return x * lax.rsqrt(jnp.mean(x * x, axis=-1, keepdims=True) + NORM_EPS)


def _mod_kernel(c_ref, w_ref, b_ref, o_ref):
    o_ref[...] = _dot(_silu(c_ref[...]), w_ref[...]) + b_ref[...]


def _modulation(c, ada_w, ada_b):
    batch, d = c.shape
    n = ada_w.shape[1]
    return pl.pallas_call(
        _mod_kernel,
        grid=(n // d,),
        in_specs=[
            pl.BlockSpec((batch, d), lambda j: (0, 0)),
            pl.BlockSpec((d, d), lambda j: (0, j)),
            pl.BlockSpec((1, d), lambda j: (0, j)),
        ],
        out_specs=pl.BlockSpec((batch, d), lambda j: (0, j)),
        out_shape=jax.ShapeDtypeStruct((batch, n), F32),
        name="modulation",
    )(c, ada_w, ada_b.reshape(1, n))


def _rope_kernel(cos_ref, sin_ref):
    rows = cos_ref.shape[0]
    pos = (pl.program_id(0) * rows + lax.broadcasted_iota(jnp.int32, (rows, HEAD_DIM), 0)).astype(F32)
    lane = lax.broadcasted_iota(jnp.int32, (rows, HEAD_DIM), 1)
    half = HEAD_DIM // 2
    freq = jnp.where(lane < half, lane, lane - half).astype(F32)
    inv = jnp.exp(freq * (-2.0 * math.log(ROPE_BASE) / HEAD_DIM))
    ang = pos * inv
    cos_ref[...] = jnp.cos(ang)
    sin_ref[...] = jnp.where(lane < half, -jnp.sin(ang), jnp.sin(ang))


def _rope_tables(seq):
    spec = pl.BlockSpec((ROPE_BLOCK, HEAD_DIM), lambda i: (i, 0))
    shape = jax.ShapeDtypeStruct((seq, HEAD_DIM), F32)
    return pl.pallas_call(
        _rope_kernel, grid=(seq // ROPE_BLOCK,), in_specs=[], out_specs=[spec, spec],
        out_shape=[shape, shape], name="rope_tables")()


def _mix_kernel(x_ref, mod_ref, nw_ref, w_ref, cos_ref, sin_ref, cw_ref, alog_ref, dtb_ref,
                rnw_ref, gnw_ref, o_ref,
                ret_ref, xpad_ref, gz_ref, gab_ref, rstate_ref, gstate_ref,
                dmask_ref, qdec_ref, kdec_ref):
    tb = x_ref.shape[0]
    n_chunks = tb // CHUNK
    log_gamma = [math.log(1.0 - 2.0 ** (-5.0 - h)) for h in range(HEADS)]
    key_scale = HEAD_DIM ** -0.5

    @pl.when((pl.program_id(0) == 0) & (pl.program_id(1) == 0))
    def _():
        row = lax.broadcasted_iota(jnp.int32, (tb, tb), 0)
        col = lax.broadcasted_iota(jnp.int32, (tb, tb), 1)
        dist = jnp.abs(row - col).astype(F32)
        visible = (col // CHUNK) <= (row // CHUNK)
        ridx = lax.broadcasted_iota(jnp.int32, (tb, HEAD_DIM), 0).astype(F32)
        for h in range(HEADS):
            dmask_ref[h] = jnp.where(visible, jnp.exp(log_gamma[h] * dist) * key_scale, 0.0)
            qdec_ref[h] = jnp.exp(log_gamma[h] * (ridx + 1.0))
            kdec_ref[h] = jnp.exp(log_gamma[h] * (tb - 1.0 - ridx)) * key_scale

    @pl.when(pl.program_id(1) == 0)
    def _():
        rstate_ref[...] = jnp.zeros_like(rstate_ref)
        gstate_ref[...] = jnp.zeros_like(gstate_ref)
        xpad_ref[0:CONV_PAD, :] = jnp.zeros((CONV_PAD, CONV_CH), F32)

    shift = mod_ref[0:1, :]
    scale = mod_ref[1:2, :]
    h_in = (_rms(x_ref[...]) * (nw_ref[...] * (1.0 + scale)) + shift).astype(BF16)

    def project(dest_ref, row0, col0, width):
        for c in range(0, width, PROJ_TILE):
            n = min(PROJ_TILE, width - c)
            w_tile = _unpack_rows(w_ref[:, col0 + c:col0 + c + n])
            dest_ref[row0:row0 + tb, c:c + n] = jnp.dot(h_in, w_tile, preferred_element_type=F32)

    project(ret_ref, 0, COL_RET, COL_CONV - COL_RET)
    project(xpad_ref, CONV_PAD, COL_CONV, CONV_CH)
    project(gz_ref, 0, COL_GZ, GROUP_W)
    project(gab_ref, 0, COL_GAB, GAB_W)

    heads = range(HEADS)

    def head_cols(group, h):
        return slice(group * GROUP_W + h * HEAD_DIM, group * GROUP_W + (h + 1) * HEAD_DIM)

    cos = cos_ref[...]
    sin = sin_ref[...]
    rq, rk, rvb = [], [], []
    for h in heads:
        q = ret_ref[:, head_cols(0, h)]
        k = ret_ref[:, head_cols(1, h)]
        rq.append(q * cos + pltpu.roll(q, HEAD_DIM // 2, 1) * sin)
        rk.append(k * cos + pltpu.roll(k, HEAD_DIM // 2, 1) * sin)
        rvb.append(ret_ref[:, head_cols(2, h)].astype(BF16))
    scores = [_dot_nt(rq[h], rk[h]) * dmask_ref[h] for h in heads]
    rstates = [rstate_ref[h] for h in heads]
    ro = [_dot(scores[h], rvb[h]) + _dot(rq[h] * qdec_ref[h], rstates[h]) for h in heads]
    for h in heads:
        rstate_ref[h] = rstates[h] * math.exp(log_gamma[h] * tb) + _dot_tn(rk[h] * kdec_ref[h], rvb[h])
    for h in heads:
        mu = jnp.mean(ro[h], axis=-1, keepdims=True)
        d = ro[h] - mu
        var = jnp.mean(d * d, axis=-1, keepdims=True)
        y = d * lax.rsqrt(var + NORM_EPS) * rnw_ref[:, head_cols(0, h)] * _silu(ret_ref[:, head_cols(3, h)])
        o_ref[:, head_cols(0, h)] = y.astype(o_ref.dtype)

    conv = xpad_ref[CONV_PAD - CONV_WIDTH + 1:CONV_PAD - CONV_WIDTH + 1 + tb, :] * cw_ref[0:1, :]
    for w in range(1, CONV_WIDTH):
        start = CONV_PAD - CONV_WIDTH + 1 + w
        conv = conv + xpad_ref[start:start + tb, :] * cw_ref[w:w + 1, :]
    xpad_ref[0:CONV_PAD, :] = xpad_ref[tb:tb + CONV_PAD, :]
    qkv = _silu(conv)

    gab = gab_ref[...]
    g = -jnp.exp(alog_ref[...]) * _softplus(gab + dtb_ref[...])
    beta = _sigmoid(gab)
    in_chunk = lax.broadcasted_iota(jnp.int32, (tb, GAB_W), 0) % CHUNK
    gc = g
    step = 1
    while step < CHUNK:
        gc = gc + jnp.where(in_chunk >= step, pltpu.roll(gc, step, 0), 0.0)
        step *= 2

    gq, gk, gk_b, gv, gch, bh, e_gc, k_dec = [], [], [], [], [], [], [], []
    for h in heads:
        qh = qkv[:, head_cols(0, h)]
        kh = qkv[:, head_cols(1, h)]
        gq.append(qh * lax.rsqrt(jnp.sum(qh * qh, axis=-1, keepdims=True) + NORM_EPS) * (HEAD_DIM ** -0.5))
        gk.append(kh * lax.rsqrt(jnp.sum(kh * kh, axis=-1, keepdims=True) + NORM_EPS))
        gk_b.append(gk[h].astype(BF16))
        gv.append(qkv[:, head_cols(2, h)])
        gch.append(jnp.broadcast_to(gc[:, h:h + 1], (tb, HEAD_DIM)))
        bh.append(jnp.broadcast_to(beta[:, HEADS + h:HEADS + h + 1], (tb, HEAD_DIM)))
        e_gc.append(jnp.exp(gch[h]))
        g_last = jnp.concatenate(
            [jnp.broadcast_to(gch[h][(c + 1) * CHUNK - 1:(c + 1) * CHUNK, :], (CHUNK, HEAD_DIM))
             for c in range(n_chunks)], axis=0)
        k_dec.append(gk[h] * jnp.exp(g_last - gch[h]))

    pair = 2 * CHUNK
    units = [(h, p) for p in range(tb // pair) for h in heads]
    prow = lax.broadcasted_iota(jnp.int32, (pair, pair), 0)
    pcol = lax.broadcasted_iota(jnp.int32, (pair, pair), 1)
    causal = ((prow // CHUNK) == (pcol // CHUNK)) & (prow >= pcol)
    diag = prow == pcol
    eye = jnp.where(diag, 1.0, 0.0)

    def unit_rows(p):
        return slice(p * pair, (p + 1) * pair)

    kk = [_dot_nt(gk_b[h][unit_rows(p)], gk_b[h][unit_rows(p)]) for h, p in units]
    qk = [_dot_nt(gq[h][unit_rows(p)], gk_b[h][unit_rows(p)]) for h, p in units]
    decay, power, inv = [], [], []
    for u, (h, p) in enumerate(units):
        gm = gch[h][unit_rows(p)]
        decay.append(jnp.exp(jnp.where(causal, gm - gm.T, -jnp.inf)))
        power.append(jnp.where(diag, 0.0, -(bh[h][unit_rows(p)] * kk[u] * decay[u])))
        inv.append(eye + power[u])
    level = 2
    while level < CHUNK:
        power = [_dot(pw, pw) for pw in power]
        inv = [iv + _dot(iv, pw) for iv, pw in zip(inv, power)]
        level *= 2
    uw = []
    for u, (h, p) in enumerate(units):
        rows = unit_rows(p)
        b = bh[h][rows]
        rhs = jnp.concatenate([b * gv[h][rows], b * gk[h][rows] * e_gc[h][rows]], axis=1)
        uw.append(_dot(inv[u], rhs))
    a_uw = [_dot(qk[u] * decay[u], uw[u]) for u in range(len(units))]
    q_eff, k_uw = [], []
    for u, (h, p) in enumerate(units):
        rows = unit_rows(p)
        q_eff.append(gq[h][rows] * e_gc[h][rows] - a_uw[u][:, HEAD_DIM:])
        kd_t = k_dec[h][rows].T
        k_uw.append([_dot(jnp.where(pcol // CHUNK == c, kd_t, 0.0), uw[u])
                     for c in range(pair // CHUNK)])
    gstates = [gstate_ref[h] for h in heads]
    outs = [[] for _ in heads]
    for c in range(n_chunks):
        p, cl = divmod(c, pair // CHUNK)
        for h in heads:
            u = p * HEADS + h
            local = slice(cl * CHUNK, (cl + 1) * CHUNK)
            sb = gstates[h].astype(BF16)
            outs[h].append(_dot(q_eff[u][local], sb) + a_uw[u][local, :HEAD_DIM])
            chunk_decay = jnp.exp(gch[h][(c + 1) * CHUNK - 1:(c + 1) * CHUNK, :])
            gstates[h] = (gstates[h] * chunk_decay - _dot(k_uw[u][cl][:, HEAD_DIM:], sb)
                          + k_uw[u][cl][:, :HEAD_DIM])
    for h in heads:
        gstate_ref[h] = gstates[h]
        o = jnp.concatenate(outs[h], axis=0)
        y = _rms(o) * gnw_ref[...] * _silu(gz_ref[:, head_cols(0, h)])
        o_ref[:, head_cols(1, h)] = y.astype(o_ref.dtype)


def _token_mix(x, mod, norm_w, w_proj, cos_t, sin_t, conv_w, a_log_pad, dt_pad, ret_norm_w, gdn_norm_w):
    batch, seq, d = x.shape
    tb = TIME_BLOCK
    const = lambda shape: pl.BlockSpec(shape, lambda b, t: (0,) * len(shape))
    return pl.pallas_call(
        _mix_kernel,
        grid=(batch, seq // tb),
        in_specs=[
            pl.BlockSpec((None, tb, d), lambda b, t: (b, t, 0)),
            pl.BlockSpec((None, N_MOD, d), lambda b, t: (b, 0, 0)),
            const((1, d)),
            pl.BlockSpec((d // 2, PROJ_W), lambda b, t: (0, 0), pipeline_mode=pl.Buffered(1)),
            pl.BlockSpec((tb, HEAD_DIM), lambda b, t: (t, 0)),
            pl.BlockSpec((tb, HEAD_DIM), lambda b, t: (t, 0)),
            const((CONV_WIDTH, CONV_CH)),
            const((1, GAB_W)),
            const((1, GAB_W)),
            const((1, GROUP_W)),
            const((1, HEAD_DIM)),
        ],
        out_specs=pl.BlockSpec((None, tb, 2 * GROUP_W), lambda b, t: (b, t, 0)),
        out_shape=jax.ShapeDtypeStruct((batch, seq, 2 * GROUP_W), BF16),
        scratch_shapes=[
            pltpu.VMEM((tb, 4 * GROUP_W), F32),
            pltpu.VMEM((tb + CONV_PAD, CONV_CH), F32),
            pltpu.VMEM((tb, GROUP_W), F32),
            pltpu.VMEM((tb, GAB_W), F32),
            pltpu.VMEM((HEADS, HEAD_DIM, HEAD_DIM), F32),
            pltpu.VMEM((HEADS, HEAD_DIM, HEAD_DIM), F32),
            pltpu.VMEM((HEADS, tb, tb), F32),
            pltpu.VMEM((HEADS, tb, HEAD_DIM), F32),
            pltpu.VMEM((HEADS, tb, HEAD_DIM), F32),
        ],
        compiler_params=pltpu.CompilerParams(
            dimension_semantics=("arbitrary", "arbitrary"), vmem_limit_bytes=VMEM_LIMIT_BYTES),
        name="token_mix",
    )(x, mod, norm_w, w_proj, cos_t, sin_t, conv_w, a_log_pad, dt_pad, ret_norm_w, gdn_norm_w)


def _channel_kernel(x_ref, mixed_ref, mod_ref, wo_ref, nw_ref, w1_ref, w2_ref, fw_ref, o_ref):
    gate_a = mod_ref[2:3, :]
    shift = mod_ref[3:4, :]
    scale = mod_ref[4:5, :]
    gate_m = mod_ref[5:6, :]
    x1 = x_ref[...] + gate_a * jnp.dot(mixed_ref[...], _unpack_rows(wo_ref[...]), preferred_element_type=F32)
    h = (_rms(x1) * nw_ref[...] * (1.0 + scale) + shift).astype(BF16)
    d_ff = w1_ref.shape[1]
    acc = jnp.zeros(x1.shape, F32)
    for j in range(d_ff // FF_BLOCK):
        cols = slice(j * FF_BLOCK, (j + 1) * FF_BLOCK)
        a = jnp.maximum(jnp.dot(h, _unpack_rows(w1_ref[:, cols]), preferred_element_type=F32), 0.0)
        w2_rows = _unpack_rows(w2_ref[j * FF_BLOCK // 2:(j + 1) * FF_BLOCK // 2, :])
        acc = acc + jnp.dot((a * a).astype(BF16), w2_rows, preferred_element_type=F32)
    x2 = x1 + gate_m * acc
    o_ref[...] = _rms(x2) * fw_ref[...]


def _channel_mix(x, mixed, mod, w_out, norm_w, w_ff1, w_ff2, final_w):
    batch, seq, d = x.shape
    d_ff = w_ff1.shape[1]
    rb = ROW_BLOCK
    per_batch = seq // rb
    resident = lambda shape: pl.BlockSpec(shape, lambda i: (0, 0), pipeline_mode=pl.Buffered(1))
    out = pl.pallas_call(
        _channel_kernel,
        grid=(batch * per_batch,),
        in_specs=[
            pl.BlockSpec((rb, d), lambda i: (i, 0)),
            pl.BlockSpec((rb, mixed.shape[-1]), lambda i: (i, 0)),
            pl.BlockSpec((None, N_MOD, d), lambda i: (i // per_batch, 0, 0)),
            resident((mixed.shape[-1] // 2, d)),
            pl.BlockSpec((1, d), lambda i: (0, 0)),
            resident((d // 2, d_ff)),
            resident((d_ff // 2, d)),
            pl.BlockSpec((1, d), lambda i: (0, 0)),
        ],
        out_specs=pl.BlockSpec((rb, d), lambda i: (i, 0)),
        out_shape=jax.ShapeDtypeStruct((batch * seq, d), F32),
        compiler_params=pltpu.CompilerParams(
            dimension_semantics=("arbitrary",), vmem_limit_bytes=VMEM_LIMIT_BYTES),
        name="channel_mix",
    )(x.reshape(batch * seq, d), mixed.reshape(batch * seq, -1), mod, w_out, norm_w, w_ff1, w_ff2, final_w)
    return out.reshape(batch, seq, d)


def _pad_lanes(v, width):
    return jnp.pad(v, (0, width - v.shape[0])).reshape(1, width)


def kernel(x, c, ada_w, ada_b, norm_mix_w, w_in, conv_w, a_log, dt_bias, ret_norm_w, gdn_norm_w, w_out,
           norm_mlp_w, w_ff1, w_ff2, norm_final_w):
    batch, seq, d = x.shape
    assert ada_w.shape[0] == 1, "single-layer block: the final rmsnorm is fused into the channel-mix call"
    cos_t, sin_t = _rope_tables(seq)
    mod = _modulation(c, ada_w[0], ada_b[0]).reshape(batch, N_MOD, d)
    mixed = _token_mix(
        x, mod, norm_mix_w[0].reshape(1, d), _pack_rows(w_in[0], PROJ_W), cos_t, sin_t, conv_w[0],
        _pad_lanes(a_log[0], GAB_W), _pad_lanes(dt_bias[0], GAB_W),
        ret_norm_w[0].reshape(1, GROUP_W), gdn_norm_w[0].reshape(1, HEAD_DIM))
    return _channel_mix(x, mixed, mod, _pack_rows(w_out[0]), norm_mlp_w[0].reshape(1, d),
                        _pack_rows(w_ff1[0]), _pack_rows(w_ff2[0]), norm_final_w.reshape(1, d))
```

```python
import math

import jax
import jax.numpy as jnp
from jax import lax
from jax.experimental import pallas as pl
from jax.experimental.pallas import tpu as pltpu

CHUNK = 64
HEADS = 4
HEAD_DIM = 128
GROUP_W = HEADS * HEAD_DIM
CONV_WIDTH = 4
CONV_CH = 3 * GROUP_W
CONV_PAD = 8
ROPE_BASE = 10000.0
NORM_EPS = 1e-6
N_MOD = 6

TIME_BLOCK = 256
PROJ_TILE = 256
ROW_BLOCK = 512
FF_BLOCK = 1024
ROPE_BLOCK = 512
LANES = 128
PACK_BLOCK_BYTES = 6 * 1024 * 1024
VMEM_LIMIT_BYTES = 48 * 1024 * 1024

COL_RET = 0
COL_CONV = 4 * GROUP_W
COL_GZ = COL_CONV + CONV_CH
COL_GAB = COL_GZ + GROUP_W
GAB_W = LANES
PROJ_W = COL_GAB + GAB_W

NT_DIMS = (((1,), (1,)), ((), ()))
TN_DIMS = (((0,), (0,)), ((), ()))

BF16 = jnp.bfloat16
F32 = jnp.float32


def _dot(a, b):
    return jnp.dot(a.astype(BF16), b.astype(BF16), preferred_element_type=F32)


def _dot_nt(a, b):
    return lax.dot_general(a.astype(BF16), b.astype(BF16), NT_DIMS, preferred_element_type=F32)


def _dot_tn(a, b):
    return lax.dot_general(a.astype(BF16), b.astype(BF16), TN_DIMS, preferred_element_type=F32)


def _pack_kernel(n_valid, w_ref, o_ref):
    w = w_ref[...]
    if n_valid % w.shape[1]:
        col = pl.program_id(0) * w.shape[1] + lax.broadcasted_iota(jnp.int32, w.shape, 1)
        w = jnp.where(col < n_valid, w, 0.0)
    o_ref[...] = pltpu.bitcast(w.astype(BF16), jnp.uint32)


def _pack_rows(w, n_out=None):
    _, k, n = w.shape
    n_out = n if n_out is None else n_out
    tile = max(t for t in range(LANES, n_out + 1, LANES)
               if n_out % t == 0 and (k * t * 4 <= PACK_BLOCK_BYTES or t == LANES))
    return pl.pallas_call(
        lambda w_ref, o_ref: _pack_kernel(n, w_ref, o_ref),
        grid=(n_out // tile,),
        in_specs=[pl.BlockSpec((None, k, tile), lambda j: (0, 0, j))],
        out_specs=pl.BlockSpec((k // 2, tile), lambda j: (0, j)),
        out_shape=jax.ShapeDtypeStruct((k // 2, n_out), jnp.uint32),
        name="pack_weight",
    )(w)


def _unpack_rows(w):
    return pltpu.bitcast(w, BF16)


def _sigmoid(x):
    return 1.0 / (1.0 + jnp.exp(-x))


def _silu(x):
    return x * _sigmoid(x)


def _softplus(x):
    return jnp.maximum(x, 0.0) + jnp.log(1.0 + jnp.exp(-jnp.abs(x)))


def _rms(x):
    return x * lax.rsqrt(jnp.mean(x * x, axis=-1, keepdims=True) + NORM_EPS)


def _mod_kernel(c_ref, w_ref, b_ref, o_ref):
    o_ref[...] = _dot(_silu(c_ref[...]), w_ref[...]) + b_ref[...]


def _modulation(c, ada_w, ada_b):
    batch, d = c.shape
    n = ada_w.shape[1]
    return pl.pallas_call(
        _mod_kernel,
        grid=(n // d,),
        in_specs=[
            pl.BlockSpec((batch, d), lambda j: (0, 0)),
            pl.BlockSpec((d, d), lambda j: (0, j)),
            pl.BlockSpec((1, d), lambda j: (0, j)),
        ],
        out_specs=pl.BlockSpec((batch, d), lambda j: (0, j)),
        out_shape=jax.ShapeDtypeStruct((batch, n), F32),
        name="modulation",
    )(c, ada_w, ada_b.reshape(1, n))


def _rope_kernel(cos_ref, sin_ref):
    rows = cos_ref.shape[0]
    pos = (pl.program_id(0) * rows + lax.broadcasted_iota(jnp.int32, (rows, HEAD_DIM), 0)).astype(F32)
    lane = lax.broadcasted_iota(jnp.int32, (rows, HEAD_DIM), 1)
    half = HEAD_DIM // 2
    freq = jnp.where(lane < half, lane, lane - half).astype(F32)
    inv = jnp.exp(freq * (-2.0 * math.log(ROPE_BASE) / HEAD_DIM))
    ang = pos * inv
    cos_ref[...] = jnp.cos(ang)
    sin_ref[...] = jnp.where(lane < half, -jnp.sin(ang), jnp.sin(ang))


def _rope_tables(seq):
    spec = pl.BlockSpec((ROPE_BLOCK, HEAD_DIM), lambda i: (i, 0))
    shape = jax.ShapeDtypeStruct((seq, HEAD_DIM), F32)
    return pl.pallas_call(
        _rope_kernel, grid=(seq // ROPE_BLOCK,), in_specs=[], out_specs=[spec, spec],
        out_shape=[shape, shape], name="rope_tables")()


def _mix_kernel(x_ref, mod_ref, nw_ref, w_ref, cos_ref, sin_ref, cw_ref, alog_ref, dtb_ref,
                rnw_ref, gnw_ref, o_ref,
                ret_ref, xpad_ref, gz_ref, gab_ref, rstate_ref, gstate_ref,
                dmask_ref, qdec_ref, kdec_ref):
    tb = x_ref.shape[0]
    n_chunks = tb // CHUNK
    log_gamma = [math.log(1.0 - 2.0 ** (-5.0 - h)) for h in range(HEADS)]
    key_scale = HEAD_DIM ** -0.5

    @pl.when((pl.program_id(0) == 0) & (pl.program_id(1) == 0))
    def _():
        row = lax.broadcasted_iota(jnp.int32, (tb, tb), 0)
        col = lax.broadcasted_iota(jnp.int32, (tb, tb), 1)
        dist = jnp.abs(row - col).astype(F32)
        visible = (col // CHUNK) <= (row // CHUNK)
        ridx = lax.broadcasted_iota(jnp.int32, (tb, HEAD_DIM), 0).astype(F32)
        for h in range(HEADS):
            dmask_ref[h] = jnp.where(visible, jnp.exp(log_gamma[h] * dist) * key_scale, 0.0)
            qdec_ref[h] = jnp.exp(log_gamma[h] * (ridx + 1.0))
            kdec_ref[h] = jnp.exp(log_gamma[h] * (tb - 1.0 - ridx)) * key_scale

    @pl.when(pl.program_id(1) == 0)
    def _():
        rstate_ref[...] = jnp.zeros_like(rstate_ref)
        gstate_ref[...] = jnp.zeros_like(gstate_ref)
        xpad_ref[0:CONV_PAD, :] = jnp.zeros((CONV_PAD, CONV_CH), F32)

    shift = mod_ref[0:1, :]
    scale = mod_ref[1:2, :]
    h_in = (_rms(x_ref[...]) * (nw_ref[...] * (1.0 + scale)) + shift).astype(BF16)

    def project(dest_ref, row0, col0, width):
        for c in range(0, width, PROJ_TILE):
            n = min(PROJ_TILE, width - c)
            w_tile = _unpack_rows(w_ref[:, col0 + c:col0 + c + n])
            dest_ref[row0:row0 + tb, c:c + n] = jnp.dot(h_in, w_tile, preferred_element_type=F32)

    project(gab_ref, 0, COL_GAB, GAB_W)
    project(xpad_ref, CONV_PAD, COL_CONV, CONV_CH)
    project(ret_ref, 0, COL_RET, COL_CONV - COL_RET)
    project(gz_ref, 0, COL_GZ, GROUP_W)

    heads = range(HEADS)

    def head_cols(group, h):
        return slice(group * GROUP_W + h * HEAD_DIM, group * GROUP_W + (h + 1) * HEAD_DIM)

    conv = xpad_ref[CONV_PAD - CONV_WIDTH + 1:CONV_PAD - CONV_WIDTH + 1 + tb, :] * cw_ref[0:1, :]
    for w in range(1, CONV_WIDTH):
        start = CONV_PAD - CONV_WIDTH + 1 + w
        conv = conv + xpad_ref[start:start + tb, :] * cw_ref[w:w + 1, :]
    xpad_ref[0:CONV_PAD, :] = xpad_ref[tb:tb + CONV_PAD, :]
    qkv = _silu(conv)

    gab = gab_ref[...]
    g = -jnp.exp(alog_ref[...]) * _softplus(gab + dtb_ref[...])
    beta = _sigmoid(gab)
    in_chunk = lax.broadcasted_iota(jnp.int32, (tb, GAB_W), 0) % CHUNK
    gc = g
    step = 1
    while step < CHUNK:
        gc = gc + jnp.where(in_chunk >= step, pltpu.roll(gc, step, 0), 0.0)
        step *= 2

    gq, gk, gk_b, gv, gch, bh, e_gc, k_dec = [], [], [], [], [], [], [], []
    for h in heads:
        qh = qkv[:, head_cols(0, h)]
        kh = qkv[:, head_cols(1, h)]
        gq.append(qh * lax.rsqrt(jnp.sum(qh * qh, axis=-1, keepdims=True) + NORM_EPS) * (HEAD_DIM ** -0.5))
        gk.append(kh * lax.rsqrt(jnp.sum(kh * kh, axis=-1, keepdims=True) + NORM_EPS))
        gk_b.append(gk[h].astype(BF16))
        gv.append(qkv[:, head_cols(2, h)])
        gch.append(jnp.broadcast_to(gc[:, h:h + 1], (tb, HEAD_DIM)))
        bh.append(jnp.broadcast_to(beta[:, HEADS + h:HEADS + h + 1], (tb, HEAD_DIM)))
        e_gc.append(jnp.exp(gch[h]))
        g_last = jnp.concatenate(
            [jnp.broadcast_to(gch[h][(c + 1) * CHUNK - 1:(c + 1) * CHUNK, :], (CHUNK, HEAD_DIM))
             for c in range(n_chunks)], axis=0)
        k_dec.append(gk[h] * jnp.exp(g_last - gch[h]))

    cos = cos_ref[...]
    sin = sin_ref[...]
    rq, rk, rvb = [], [], []
    for h in heads:
        q = ret_ref[:, head_cols(0, h)]
        k = ret_ref[:, head_cols(1, h)]
        rq.append(q * cos + pltpu.roll(q, HEAD_DIM // 2, 1) * sin)
        rk.append(k * cos + pltpu.roll(k, HEAD_DIM // 2, 1) * sin)
        rvb.append(ret_ref[:, head_cols(2, h)].astype(BF16))
    scores = [_dot_nt(rq[h], rk[h]) * dmask_ref[h] for h in heads]
    rstates = [rstate_ref[h] for h in heads]
    ro = [_dot(scores[h], rvb[h]) + _dot(rq[h] * qdec_ref[h], rstates[h]) for h in heads]
    for h in heads:
        rstate_ref[h] = rstates[h] * math.exp(log_gamma[h] * tb) + _dot_tn(rk[h] * kdec_ref[h], rvb[h])
    for h in heads:
        mu = jnp.mean(ro[h], axis=-1, keepdims=True)
        d = ro[h] - mu
        var = jnp.mean(d * d, axis=-1, keepdims=True)
        y = d * lax.rsqrt(var + NORM_EPS) * rnw_ref[:, head_cols(0, h)] * _silu(ret_ref[:, head_cols(3, h)])
        o_ref[:, head_cols(0, h)] = y.astype(o_ref.dtype)

    pair = 2 * CHUNK
    units = [(h, p) for p in range(tb // pair) for h in heads]
    prow = lax.broadcasted_iota(jnp.int32, (pair, pair), 0)
    pcol = lax.broadcasted_iota(jnp.int32, (pair, pair), 1)
    causal = ((prow // CHUNK) == (pcol // CHUNK)) & (prow >= pcol)
    diag = prow == pcol
    eye = jnp.where(diag, 1.0, 0.0)

    def unit_rows(p):
        return slice(p * pair, (p + 1) * pair)

    kk = [_dot_nt(gk_b[h][unit_rows(p)], gk_b[h][unit_rows(p)]) for h, p in units]
    qk = [_dot_nt(gq[h][unit_rows(p)], gk_b[h][unit_rows(p)]) for h, p in units]
    decay, power, inv = [], [], []
    for u, (h, p) in enumerate(units):
        gm = gch[h][unit_rows(p)]
        decay.append(jnp.exp(jnp.where(causal, gm - gm.T, -jnp.inf)))
        power.append(jnp.where(diag, 0.0, -(bh[h][unit_rows(p)] * kk[u] * decay[u])))
        inv.append(eye + power[u])
    level = 2
    while level < CHUNK:
        power = [_dot(pw, pw) for pw in power]
        inv = [iv + _dot(iv, pw) for iv, pw in zip(inv, power)]
        level *= 2
    uw = []
    for u, (h, p) in enumerate(units):
        rows = unit_rows(p)
        b = bh[h][rows]
        rhs = jnp.concatenate([b * gv[h][rows], b * gk[h][rows] * e_gc[h][rows]], axis=1)
        uw.append(_dot(inv[u], rhs))
    a_uw = [_dot(qk[u] * decay[u], uw[u]) for u in range(len(units))]
    q_eff, k_uw = [], []
    for u, (h, p) in enumerate(units):
        rows = unit_rows(p)
        q_eff.append(gq[h][rows] * e_gc[h][rows] - a_uw[u][:, HEAD_DIM:])
        kd_t = k_dec[h][rows].T
        k_uw.append([_dot(jnp.where(pcol // CHUNK == c, kd_t, 0.0), uw[u])
                     for c in range(pair // CHUNK)])
    gstates = [gstate_ref[h] for h in heads]
    outs = [[] for _ in heads]
    for c in range(n_chunks):
        p, cl = divmod(c, pair // CHUNK)
        for h in heads:
            u = p * HEADS + h
            local = slice(cl * CHUNK, (cl + 1) * CHUNK)
            sb = gstates[h].astype(BF16)
            outs[h].append(_dot(q_eff[u][local], sb) + a_uw[u][local, :HEAD_DIM])
            chunk_decay = jnp.exp(gch[h][(c + 1) * CHUNK - 1:(c + 1) * CHUNK, :])
            gstates[h] = (gstates[h] * chunk_decay - _dot(k_uw[u][cl][:, HEAD_DIM:], sb)
                          + k_uw[u][cl][:, :HEAD_DIM])
    for h in heads:
        gstate_ref[h] = gstates[h]
        o = jnp.concatenate(outs[h], axis=0)
        y = _rms(o) * gnw_ref[...] * _silu(gz_ref[:, head_cols(0, h)])
        o_ref[:, head_cols(1, h)] = y.astype(o_ref.dtype)


def _token_mix(x, mod, norm_w, w_proj, cos_t, sin_t, conv_w, a_log_pad, dt_pad, ret_norm_w, gdn_norm_w):
    batch, seq, d = x.shape
    tb = TIME_BLOCK
    const = lambda shape: pl.BlockSpec(shape, lambda b, t: (0,) * len(shape))
    return pl.pallas_call(
        _mix_kernel,
        grid=(batch, seq // tb),
        in_specs=[
            pl.BlockSpec((None, tb, d), lambda b, t: (b, t, 0)),
            pl.BlockSpec((None, N_MOD, d), lambda b, t: (b, 0, 0)),
            const((1, d)),
            pl.BlockSpec((d // 2, PROJ_W), lambda b, t: (0, 0), pipeline_mode=pl.Buffered(1)),
            pl.BlockSpec((tb, HEAD_DIM), lambda b, t: (t, 0)),
            pl.BlockSpec((tb, HEAD_DIM), lambda b, t: (t, 0)),
            const((CONV_WIDTH, CONV_CH)),
            const((1, GAB_W)),
            const((1, GAB_W)),
            const((1, GROUP_W)),
            const((1, HEAD_DIM)),
        ],
        out_specs=pl.BlockSpec((None, tb, 2 * GROUP_W), lambda b, t: (b, t, 0)),
        out_shape=jax.ShapeDtypeStruct((batch, seq, 2 * GROUP_W), BF16),
        scratch_shapes=[
            pltpu.VMEM((tb, 4 * GROUP_W), F32),
            pltpu.VMEM((tb + CONV_PAD, CONV_CH), F32),
            pltpu.VMEM((tb, GROUP_W), F32),
            pltpu.VMEM((tb, GAB_W), F32),
            pltpu.VMEM((HEADS, HEAD_DIM, HEAD_DIM), F32),
            pltpu.VMEM((HEADS, HEAD_DIM, HEAD_DIM), F32),
            pltpu.VMEM((HEADS, tb, tb), F32),
            pltpu.VMEM((HEADS, tb, HEAD_DIM), F32),
            pltpu.VMEM((HEADS, tb, HEAD_DIM), F32),
        ],
        compiler_params=pltpu.CompilerParams(
            dimension_semantics=("arbitrary", "arbitrary"), vmem_limit_bytes=VMEM_LIMIT_BYTES),
        name="token_mix",
    )(x, mod, norm_w, w_proj, cos_t, sin_t, conv_w, a_log_pad, dt_pad, ret_norm_w, gdn_norm_w)


def _channel_kernel(x_ref, mixed_ref, mod_ref, wo_ref, nw_ref, w1_ref, w2_ref, fw_ref, o_ref):
    gate_a = mod_ref[2:3, :]
    shift = mod_ref[3:4, :]
    scale = mod_ref[4:5, :]
    gate_m = mod_ref[5:6, :]
    x1 = x_ref[...] + gate_a * jnp.dot(mixed_ref[...], _unpack_rows(wo_ref[...]), preferred_element_type=F32)
    h = (_rms(x1) * nw_ref[...] * (1.0 + scale) + shift).astype(BF16)
    d_ff = w1_ref.shape[1]
    acc = jnp.zeros(x1.shape, F32)
    for j in range(d_ff // FF_BLOCK):
        cols = slice(j * FF_BLOCK, (j + 1) * FF_BLOCK)
        a = jnp.maximum(jnp.dot(h, _unpack_rows(w1_ref[:, cols]), preferred_element_type=F32), 0.0)
        w2_rows = _unpack_rows(w2_ref[j * FF_BLOCK // 2:(j + 1) * FF_BLOCK // 2, :])
        acc = acc + jnp.dot((a * a).astype(BF16), w2_rows, preferred_element_type=F32)
    x2 = x1 + gate_m * acc
    o_ref[...] = _rms(x2) * fw_ref[...]


def _channel_mix(x, mixed, mod, w_out, norm_w, w_ff1, w_ff2, final_w):
    batch, seq, d = x.shape
    d_ff = w_ff1.shape[1]
    rb = ROW_BLOCK
    per_batch = seq // rb
    resident = lambda shape: pl.BlockSpec(shape, lambda i: (0, 0), pipeline_mode=pl.Buffered(1))
    out = pl.pallas_call(
        _channel_kernel,
        grid=(batch * per_batch,),
        in_specs=[
            pl.BlockSpec((rb, d), lambda i: (i, 0)),
            pl.BlockSpec((rb, mixed.shape[-1]), lambda i: (i, 0)),
            pl.BlockSpec((None, N_MOD, d), lambda i: (i // per_batch, 0, 0)),
            resident((mixed.shape[-1] // 2, d)),
            pl.BlockSpec((1, d), lambda i: (0, 0)),
            resident((d // 2, d_ff)),
            resident((d_ff // 2, d)),
            pl.BlockSpec((1, d), lambda i: (0, 0)),
        ],
        out_specs=pl.BlockSpec((rb, d), lambda i: (i, 0)),
        out_shape=jax.ShapeDtypeStruct((batch * seq, d), F32),
        compiler_params=pltpu.CompilerParams(
            dimension_semantics=("arbitrary",), vmem_limit_bytes=VMEM_LIMIT_BYTES),
        name="channel_mix",
    )(x.reshape(batch * seq, d), mixed.reshape(batch * seq, -1), mod, w_out, norm_w, w_ff1, w_ff2, final_w)
    return out.reshape(batch, seq, d)


def _pad_lanes(v, width):
    return jnp.pad(v, (0, width - v.shape[0])).reshape(1, width)


def kernel(x, c, ada_w, ada_b, norm_mix_w, w_in, conv_w, a_log, dt_bias, ret_norm_w, gdn_norm_w, w_out,
           norm_mlp_w, w_ff1, w_ff2, norm_final_w):
    batch, seq, d = x.shape
    assert ada_w.shape[0] == 1, "single-layer block: the final rmsnorm is fused into the channel-mix call"
    cos_t, sin_t = _rope_tables(seq)
    mod = _modulation(c, ada_w[0], ada_b[0]).reshape(batch, N_MOD, d)
    mixed = _token_mix(
        x, mod, norm_mix_w[0].reshape(1, d), _pack_rows(w_in, PROJ_W), cos_t, sin_t, conv_w[0],
        _pad_lanes(a_log[0], GAB_W), _pad_lanes(dt_bias[0], GAB_W),
        ret_norm_w[0].reshape(1, GROUP_W), gdn_norm_w[0].reshape(1, HEAD_DIM))
    return _channel_mix(x, mixed, mod, _pack_rows(w_out), norm_mlp_w[0].reshape(1, d),
                        _pack_rows(w_ff1), _pack_rows(w_ff2), norm_final_w.reshape(1, d))
```

```python
import math

import jax
import jax.numpy as jnp
from jax import lax
from jax.experimental import pallas as pl
from jax.experimental.pallas import tpu as pltpu

CHUNK = 64
HEADS = 4
HEAD_DIM = 128
GROUP_W = HEADS * HEAD_DIM
CONV_WIDTH = 4
CONV_CH = 3 * GROUP_W
CONV_PAD = 8
ROPE_BASE = 10000.0
NORM_EPS = 1e-6
N_MOD = 6

TIME_BLOCK = 256
PROJ_TILE = 256
ROW_BLOCK = 512
FF_BLOCK = 1024
ROPE_BLOCK = 512
LANES = 128
PACK_BLOCK_BYTES = 6 * 1024 * 1024
VMEM_LIMIT_BYTES = 48 * 1024 * 1024

COL_RET = 0
COL_CONV = 4 * GROUP_W
COL_GZ = COL_CONV + CONV_CH
COL_GAB = COL_GZ + GROUP_W
GAB_W = LANES
PROJ_W = COL_GAB + GAB_W

NT_DIMS = (((1,), (1,)), ((), ()))
TN_DIMS = (((0,), (0,)), ((), ()))

BF16 = jnp.bfloat16
F32 = jnp.float32


def _dot(a, b):
    return jnp.dot(a.astype(BF16), b.astype(BF16), preferred_element_type=F32)


def _dot_nt(a, b):
    return lax.dot_general(a.astype(BF16), b.astype(BF16), NT_DIMS, preferred_element_type=F32)


def _dot_tn(a, b):
    return lax.dot_general(a.astype(BF16), b.astype(BF16), TN_DIMS, preferred_element_type=F32)


def _pack_kernel(n_valid, w_ref, o_ref):
    w = w_ref[...]
    if n_valid % w.shape[1]:
        col = pl.program_id(0) * w.shape[1] + lax.broadcasted_iota(jnp.int32, w.shape, 1)
        w = jnp.where(col < n_valid, w, 0.0)
    o_ref[...] = pltpu.bitcast(w.astype(BF16), jnp.uint32)


def _pack_rows(w, n_out=None):
    _, k, n = w.shape
    n_out = n if n_out is None else n_out
    tile = max(t for t in range(LANES, n_out + 1, LANES)
               if n_out % t == 0 and (k * t * 4 <= PACK_BLOCK_BYTES or t == LANES))
    return pl.pallas_call(
        lambda w_ref, o_ref: _pack_kernel(n, w_ref, o_ref),
        grid=(n_out // tile,),
        in_specs=[pl.BlockSpec((None, k, tile), lambda j: (0, 0, j))],
        out_specs=pl.BlockSpec((k // 2, tile), lambda j: (0, j)),
        out_shape=jax.ShapeDtypeStruct((k // 2, n_out), jnp.uint32),
        name="pack_weight",
    )(w)


def _unpack_rows(w):
    return pltpu.bitcast(w, BF16)


def _sigmoid(x):
    return 1.0 / (1.0 + jnp.exp(-x))


def _silu(x):
    return x * _sigmoid(x)


def _softplus(x):
    return jnp.maximum(x, 0.0) + jnp.log(1.0 + jnp.exp(-jnp.abs(x)))


def _rms(x):
    return x * lax.rsqrt(jnp.mean(x * x, axis=-1, keepdims=True) + NORM_EPS)


def _mod_kernel(c_ref, w_ref, b_ref, o_ref):
    o_ref[...] = _dot(_silu(c_ref[...]), w_ref[...]) + b_ref[...]


def _modulation(c, ada_w, ada_b):
    batch, d = c.shape
    n = ada_w.shape[1]
    return pl.pallas_call(
        _mod_kernel,
        grid=(n // d,),
        in_specs=[
            pl.BlockSpec((batch, d), lambda j: (0, 0)),
            pl.BlockSpec((d, d), lambda j: (0, j)),
            pl.BlockSpec((1, d), lambda j: (0, j)),
        ],
        out_specs=pl.BlockSpec((batch, d), lambda j: (0, j)),
        out_shape=jax.ShapeDtypeStruct((batch, n), F32),
        name="modulation",
    )(c, ada_w, ada_b.reshape(1, n))


def _rope_kernel(cos_ref, sin_ref):
    rows = cos_ref.shape[0]
    pos = (pl.program_id(0) * rows + lax.broadcasted_iota(jnp.int32, (rows, HEAD_DIM), 0)).astype(F32)
    lane = lax.broadcasted_iota(jnp.int32, (rows, HEAD_DIM), 1)
    half = HEAD_DIM // 2
    freq = jnp.where(lane < half, lane, lane - half).astype(F32)
    inv = jnp.exp(freq * (-2.0 * math.log(ROPE_BASE) / HEAD_DIM))
    ang = pos * inv
    cos_ref[...] = jnp.cos(ang)
    sin_ref[...] = jnp.where(lane < half, -jnp.sin(ang), jnp.sin(ang))


def _rope_tables(seq):
    spec = pl.BlockSpec((ROPE_BLOCK, HEAD_DIM), lambda i: (i, 0))
    shape = jax.ShapeDtypeStruct((seq, HEAD_DIM), F32)
    return pl.pallas_call(
        _rope_kernel, grid=(seq // ROPE_BLOCK,), in_specs=[], out_specs=[spec, spec],
        out_shape=[shape, shape], name="rope_tables")()


def _mix_kernel(x_ref, mod_ref, nw_ref, w_ref, cos_ref, sin_ref, cw_ref, alog_ref, dtb_ref,
                rnw_ref, gnw_ref, o_ref,
                ret_a, xpad_a, gz_a, gab_a, ret_b, xpad_b, gz_b, gab_b, rstate_ref, gstate_ref,
                dmask_ref, qdec_ref, kdec_ref):
    tb = ret_a.shape[0]
    n_chunks = tb // CHUNK
    log_gamma = [math.log(1.0 - 2.0 ** (-5.0 - h)) for h in range(HEADS)]
    key_scale = HEAD_DIM ** -0.5
    heads = range(HEADS)
    blocks = (((ret_a, xpad_a, gz_a, gab_a), slice(0, tb)), ((ret_b, xpad_b, gz_b, gab_b), slice(tb, 2 * tb)))

    def head_cols(group, h):
        return slice(group * GROUP_W + h * HEAD_DIM, group * GROUP_W + (h + 1) * HEAD_DIM)

    @pl.when((pl.program_id(0) == 0) & (pl.program_id(1) == 0))
    def _():
        row = lax.broadcasted_iota(jnp.int32, (tb, tb), 0)
        col = lax.broadcasted_iota(jnp.int32, (tb, tb), 1)
        dist = jnp.abs(row - col).astype(F32)
        visible = (col // CHUNK) <= (row // CHUNK)
        ridx = lax.broadcasted_iota(jnp.int32, (tb, HEAD_DIM), 0).astype(F32)
        for h in range(HEADS):
            dmask_ref[h] = jnp.where(visible, jnp.exp(log_gamma[h] * dist) * key_scale, 0.0)
            qdec_ref[h] = jnp.exp(log_gamma[h] * (ridx + 1.0))
            kdec_ref[h] = jnp.exp(log_gamma[h] * (tb - 1.0 - ridx)) * key_scale

    @pl.when(pl.program_id(1) == 0)
    def _():
        rstate_ref[...] = jnp.zeros_like(rstate_ref)
        gstate_ref[...] = jnp.zeros_like(gstate_ref)
        xpad_a[0:CONV_PAD, :] = jnp.zeros((CONV_PAD, CONV_CH), F32)

    shift = mod_ref[0:1, :]
    scale = mod_ref[1:2, :]
    in_gain = nw_ref[...] * (1.0 + scale)

    def project_block(bufs, blk):
        ret_ref, xpad_ref, gz_ref, gab_ref = bufs
        h_in = (_rms(x_ref[blk, :]) * in_gain + shift).astype(BF16)

        def project(dest_ref, row0, col0, width):
            for c in range(0, width, PROJ_TILE):
                n = min(PROJ_TILE, width - c)
                w_tile = _unpack_rows(w_ref[:, col0 + c:col0 + c + n])
                dest_ref[row0:row0 + tb, c:c + n] = jnp.dot(h_in, w_tile, preferred_element_type=F32)

        project(gab_ref, 0, COL_GAB, GAB_W)
        project(xpad_ref, CONV_PAD, COL_CONV, CONV_CH)
        project(ret_ref, 0, COL_RET, COL_CONV - COL_RET)
        project(gz_ref, 0, COL_GZ, GROUP_W)

    def delta_front(bufs):
        _, xpad_ref, _, gab_ref = bufs
        conv = xpad_ref[CONV_PAD - CONV_WIDTH + 1:CONV_PAD - CONV_WIDTH + 1 + tb, :] * cw_ref[0:1, :]
        for w in range(1, CONV_WIDTH):
            start = CONV_PAD - CONV_WIDTH + 1 + w
            conv = conv + xpad_ref[start:start + tb, :] * cw_ref[w:w + 1, :]
        qkv = _silu(conv)

        gab = gab_ref[...]
        g = -jnp.exp(alog_ref[...]) * _softplus(gab + dtb_ref[...])
        beta = _sigmoid(gab)
        in_chunk = lax.broadcasted_iota(jnp.int32, (tb, GAB_W), 0) % CHUNK
        gc = g
        step = 1
        while step < CHUNK:
            gc = gc + jnp.where(in_chunk >= step, pltpu.roll(gc, step, 0), 0.0)
            step *= 2

        f = dict(gq=[], gk=[], gk_b=[], gv=[], gch=[], bh=[], e_gc=[], k_dec=[])
        for h in heads:
            qh = qkv[:, head_cols(0, h)]
            kh = qkv[:, head_cols(1, h)]
            f["gq"].append(qh * lax.rsqrt(jnp.sum(qh * qh, axis=-1, keepdims=True) + NORM_EPS) * (HEAD_DIM ** -0.5))
            gk = kh * lax.rsqrt(jnp.sum(kh * kh, axis=-1, keepdims=True) + NORM_EPS)
            f["gk"].append(gk)
            f["gk_b"].append(gk.astype(BF16))
            f["gv"].append(qkv[:, head_cols(2, h)])
            gch = jnp.broadcast_to(gc[:, h:h + 1], (tb, HEAD_DIM))
            f["gch"].append(gch)
            f["bh"].append(jnp.broadcast_to(beta[:, HEADS + h:HEADS + h + 1], (tb, HEAD_DIM)))
            f["e_gc"].append(jnp.exp(gch))
            g_last = jnp.concatenate(
                [jnp.broadcast_to(gch[(c + 1) * CHUNK - 1:(c + 1) * CHUNK, :], (CHUNK, HEAD_DIM))
                 for c in range(n_chunks)], axis=0)
            f["k_dec"].append(gk * jnp.exp(g_last - gch))
        return f

    def retention(bufs, blk):
        ret_ref = bufs[0]
        cos = cos_ref[blk, :]
        sin = sin_ref[blk, :]
        rq, rk, rvb = [], [], []
        for h in heads:
            q = ret_ref[:, head_cols(0, h)]
            k = ret_ref[:, head_cols(1, h)]
            rq.append(q * cos + pltpu.roll(q, HEAD_DIM // 2, 1) * sin)
            rk.append(k * cos + pltpu.roll(k, HEAD_DIM // 2, 1) * sin)
            rvb.append(ret_ref[:, head_cols(2, h)].astype(BF16))
        scores = [_dot_nt(rq[h], rk[h]) * dmask_ref[h] for h in heads]
        rstates = [rstate_ref[h] for h in heads]
        ro = [_dot(scores[h], rvb[h]) + _dot(rq[h] * qdec_ref[h], rstates[h]) for h in heads]
        for h in heads:
            rstate_ref[h] = rstates[h] * math.exp(log_gamma[h] * tb) + _dot_tn(rk[h] * kdec_ref[h], rvb[h])
        for h in heads:
            mu = jnp.mean(ro[h], axis=-1, keepdims=True)
            d = ro[h] - mu
            var = jnp.mean(d * d, axis=-1, keepdims=True)
            y = d * lax.rsqrt(var + NORM_EPS) * rnw_ref[:, head_cols(0, h)] * _silu(ret_ref[:, head_cols(3, h)])
            o_ref[blk, head_cols(0, h)] = y.astype(o_ref.dtype)

    project_block(*blocks[0])
    xpad_b[0:CONV_PAD, :] = xpad_a[tb:tb + CONV_PAD, :]
    project_block(*blocks[1])
    fronts = [delta_front(bufs) for bufs, _ in blocks]
    for bufs, blk in blocks:
        retention(bufs, blk)
    xpad_a[0:CONV_PAD, :] = xpad_b[tb:tb + CONV_PAD, :]

    pair = 2 * CHUNK
    units = [(i, h, p) for i in range(len(blocks)) for p in range(tb // pair) for h in heads]
    prow = lax.broadcasted_iota(jnp.int32, (pair, pair), 0)
    pcol = lax.broadcasted_iota(jnp.int32, (pair, pair), 1)
    causal = ((prow // CHUNK) == (pcol // CHUNK)) & (prow >= pcol)
    diag = prow == pcol
    eye = jnp.where(diag, 1.0, 0.0)

    def unit_rows(p):
        return slice(p * pair, (p + 1) * pair)

    def operand(name, i, h, p):
        return fronts[i][name][h][unit_rows(p)]

    kk = [_dot_nt(operand("gk_b", *u), operand("gk_b", *u)) for u in units]
    qk = [_dot_nt(operand("gq", *u), operand("gk_b", *u)) for u in units]
    decay, power, inv = [], [], []
    for n, u in enumerate(units):
        gm = operand("gch", *u)
        decay.append(jnp.exp(jnp.where(causal, gm - gm.T, -jnp.inf)))
        power.append(jnp.where(diag, 0.0, -(operand("bh", *u) * kk[n] * decay[n])))
        inv.append(eye + power[n])
    level = 2
    while level < CHUNK:
        power = [_dot(pw, pw) for pw in power]
        inv = [iv + _dot(iv, pw) for iv, pw in zip(inv, power)]
        level *= 2
    uw = []
    for n, u in enumerate(units):
        b = operand("bh", *u)
        rhs = jnp.concatenate([b * operand("gv", *u), b * operand("gk", *u) * operand("e_gc", *u)], axis=1)
        uw.append(_dot(inv[n], rhs))
    a_uw = [_dot(qk[n] * decay[n], uw[n]) for n in range(len(units))]
    q_eff, k_uw = [], []
    for n, u in enumerate(units):
        q_eff.append(operand("gq", *u) * operand("e_gc", *u) - a_uw[n][:, HEAD_DIM:])
        kd_t = operand("k_dec", *u).T
        k_uw.append([_dot(jnp.where(pcol // CHUNK == c, kd_t, 0.0), uw[n])
                     for c in range(pair // CHUNK)])
    gstates = [gstate_ref[h] for h in heads]
    for i, (bufs, blk) in enumerate(blocks):
        outs = [[] for _ in heads]
        for c in range(n_chunks):
            p, cl = divmod(c, pair // CHUNK)
            for h in heads:
                n = units.index((i, h, p))
                local = slice(cl * CHUNK, (cl + 1) * CHUNK)
                sb = gstates[h].astype(BF16)
                outs[h].append(_dot(q_eff[n][local], sb) + a_uw[n][local, :HEAD_DIM])
                gch = fronts[i]["gch"][h]
                chunk_decay = jnp.exp(gch[(c + 1) * CHUNK - 1:(c + 1) * CHUNK, :])
                gstates[h] = (gstates[h] * chunk_decay - _dot(k_uw[n][cl][:, HEAD_DIM:], sb)
                              + k_uw[n][cl][:, :HEAD_DIM])
        gz_ref = bufs[2]
        for h in heads:
            o = jnp.concatenate(outs[h], axis=0)
            y = _rms(o) * gnw_ref[...] * _silu(gz_ref[:, head_cols(0, h)])
            o_ref[blk, head_cols(1, h)] = y.astype(o_ref.dtype)
    for h in heads:
        gstate_ref[h] = gstates[h]


def _token_mix(x, mod, norm_w, w_proj, cos_t, sin_t, conv_w, a_log_pad, dt_pad, ret_norm_w, gdn_norm_w):
    batch, seq, d = x.shape
    tb = TIME_BLOCK
    step_rows = 2 * tb
    const = lambda shape: pl.BlockSpec(shape, lambda b, t: (0,) * len(shape))
    proj_bufs = [
        pltpu.VMEM((tb, 4 * GROUP_W), F32),
        pltpu.VMEM((tb + CONV_PAD, CONV_CH), F32),
        pltpu.VMEM((tb, GROUP_W), F32),
        pltpu.VMEM((tb, GAB_W), F32),
    ]
    return pl.pallas_call(
        _mix_kernel,
        grid=(batch, seq // step_rows),
        in_specs=[
            pl.BlockSpec((None, step_rows, d), lambda b, t: (b, t, 0)),
            pl.BlockSpec((None, N_MOD, d), lambda b, t: (b, 0, 0)),
            const((1, d)),
            pl.BlockSpec((d // 2, PROJ_W), lambda b, t: (0, 0), pipeline_mode=pl.Buffered(1)),
            pl.BlockSpec((step_rows, HEAD_DIM), lambda b, t: (t, 0)),
            pl.BlockSpec((step_rows, HEAD_DIM), lambda b, t: (t, 0)),
            const((CONV_WIDTH, CONV_CH)),
            const((1, GAB_W)),
            const((1, GAB_W)),
            const((1, GROUP_W)),
            const((1, HEAD_DIM)),
        ],
        out_specs=pl.BlockSpec((None, step_rows, 2 * GROUP_W), lambda b, t: (b, t, 0)),
        out_shape=jax.ShapeDtypeStruct((batch, seq, 2 * GROUP_W), BF16),
        scratch_shapes=proj_bufs + proj_bufs + [
            pltpu.VMEM((HEADS, HEAD_DIM, HEAD_DIM), F32),
            pltpu.VMEM((HEADS, HEAD_DIM, HEAD_DIM), F32),
            pltpu.VMEM((HEADS, tb, tb), F32),
            pltpu.VMEM((HEADS, tb, HEAD_DIM), F32),
            pltpu.VMEM((HEADS, tb, HEAD_DIM), F32),
        ],
        compiler_params=pltpu.CompilerParams(
            dimension_semantics=("arbitrary", "arbitrary"), vmem_limit_bytes=VMEM_LIMIT_BYTES),
        name="token_mix",
    )(x, mod, norm_w, w_proj, cos_t, sin_t, conv_w, a_log_pad, dt_pad, ret_norm_w, gdn_norm_w)


def _channel_kernel(x_ref, mixed_ref, mod_ref, wo_ref, nw_ref, w1_ref, w2_ref, fw_ref, o_ref):
    gate_a = mod_ref[2:3, :]
    shift = mod_ref[3:4, :]
    scale = mod_ref[4:5, :]
    gate_m = mod_ref[5:6, :]
    x1 = x_ref[...] + gate_a * jnp.dot(mixed_ref[...], _unpack_rows(wo_ref[...]), preferred_element_type=F32)
    h = (_rms(x1) * nw_ref[...] * (1.0 + scale) + shift).astype(BF16)
    d_ff = w1_ref.shape[1]
    acc = jnp.zeros(x1.shape, F32)
    for j in range(d_ff // FF_BLOCK):
        cols = slice(j * FF_BLOCK, (j + 1) * FF_BLOCK)
        a = jnp.maximum(jnp.dot(h, _unpack_rows(w1_ref[:, cols]), preferred_element_type=F32), 0.0)
        w2_rows = _unpack_rows(w2_ref[j * FF_BLOCK // 2:(j + 1) * FF_BLOCK // 2, :])
        acc = acc + jnp.dot((a * a).astype(BF16), w2_rows, preferred_element_type=F32)
    x2 = x1 + gate_m * acc
    o_ref[...] = _rms(x2) * fw_ref[...]


def _channel_mix(x, mixed, mod, w_out, norm_w, w_ff1, w_ff2, final_w):
    batch, seq, d = x.shape
    d_ff = w_ff1.shape[1]
    rb = ROW_BLOCK
    per_batch = seq // rb
    resident = lambda shape: pl.BlockSpec(shape, lambda i: (0, 0), pipeline_mode=pl.Buffered(1))
    out = pl.pallas_call(
        _channel_kernel,
        grid=(batch * per_batch,),
        in_specs=[
            pl.BlockSpec((rb, d), lambda i: (i, 0)),
            pl.BlockSpec((rb, mixed.shape[-1]), lambda i: (i, 0)),
            pl.BlockSpec((None, N_MOD, d), lambda i: (i // per_batch, 0, 0)),
            resident((mixed.shape[-1] // 2, d)),
            pl.BlockSpec((1, d), lambda i: (0, 0)),
            resident((d // 2, d_ff)),
            resident((d_ff // 2, d)),
            pl.BlockSpec((1, d), lambda i: (0, 0)),
        ],
        out_specs=pl.BlockSpec((rb, d), lambda i: (i, 0)),
        out_shape=jax.ShapeDtypeStruct((batch * seq, d), F32),
        compiler_params=pltpu.CompilerParams(
            dimension_semantics=("arbitrary",), vmem_limit_bytes=VMEM_LIMIT_BYTES),
        name="channel_mix",
    )(x.reshape(batch * seq, d), mixed.reshape(batch * seq, -1), mod, w_out, norm_w, w_ff1, w_ff2, final_w)
    return out.reshape(batch, seq, d)


def _pad_lanes(v, width):
    return jnp.pad(v, (0, width - v.shape[0])).reshape(1, width)


def kernel(x, c, ada_w, ada_b, norm_mix_w, w_in, conv_w, a_log, dt_bias, ret_norm_w, gdn_norm_w, w_out,
           norm_mlp_w, w_ff1, w_ff2, norm_final_w):
    batch, seq, d = x.shape
    assert ada_w.shape[0] == 1, "single-layer block: the final rmsnorm is fused into the channel-mix call"
    cos_t, sin_t = _rope_tables(seq)
    mod = _modulation(c, ada_w[0], ada_b[0]).reshape(batch, N_MOD, d)
    mixed = _token_mix(
        x, mod, norm_mix_w[0].reshape(1, d), _pack_rows(w_in, PROJ_W), cos_t, sin_t, conv_w[0],
        _pad_lanes(a_log[0], GAB_W), _pad_lanes(dt_bias[0], GAB_W),
        ret_norm_w[0].reshape(1, GROUP_W), gdn_norm_w[0].reshape(1, HEAD_DIM))
    return _channel_mix(x, mixed, mod, _pack_rows(w_out), norm_mlp_w[0].reshape(1, d),
                        _pack_rows(w_ff1), _pack_rows(w_ff2), norm_final_w.reshape(1, d))
```

```python
import math

import jax
import jax.numpy as jnp
from jax import lax
from jax.experimental import pallas as pl
from jax.experimental.pallas import tpu as pltpu

CHUNK = 64
HEADS = 4
HEAD_DIM = 128
GROUP_W = HEADS * HEAD_DIM
CONV_WIDTH = 4
CONV_CH = 3 * GROUP_W
CONV_PAD = 8
ROPE_BASE = 10000.0
NORM_EPS = 1e-6
N_MOD = 6

TIME_BLOCK = 256
PROJ_TILE = 256
ROW_BLOCK = 512
FF_BLOCK = 1024
ROPE_BLOCK = 512
LANES = 128
PACK_BLOCK_BYTES = 6 * 1024 * 1024
VMEM_LIMIT_BYTES = 48 * 1024 * 1024

COL_RET = 0
COL_CONV = 4 * GROUP_W
COL_GZ = COL_CONV + CONV_CH
COL_GAB = COL_GZ + GROUP_W
GAB_W = LANES
PROJ_W = COL_GAB + GAB_W

NT_DIMS = (((1,), (1,)), ((), ()))
TN_DIMS = (((0,), (0,)), ((), ()))

BF16 = jnp.bfloat16
F32 = jnp.float32


def _dot(a, b):
    return jnp.dot(a.astype(BF16), b.astype(BF16), preferred_element_type=F32)


def _dot_nt(a, b):
    return lax.dot_general(a.astype(BF16), b.astype(BF16), NT_DIMS, preferred_element_type=F32)


def _dot_tn(a, b):
    return lax.dot_general(a.astype(BF16), b.astype(BF16), TN_DIMS, preferred_element_type=F32)


def _pack_kernel(n_valid, w_ref, o_ref):
    w = w_ref[...]
    if n_valid % w.shape[1]:
        col = pl.program_id(0) * w.shape[1] + lax.broadcasted_iota(jnp.int32, w.shape, 1)
        w = jnp.where(col < n_valid, w, 0.0)
    o_ref[...] = pltpu.bitcast(w.astype(BF16), jnp.uint32)


def _pack_rows(w, n_out=None):
    _, k, n = w.shape
    n_out = n if n_out is None else n_out
    tile = max(t for t in range(LANES, n_out + 1, LANES)
               if n_out % t == 0 and (k * t * 4 <= PACK_BLOCK_BYTES or t == LANES))
    return pl.pallas_call(
        lambda w_ref, o_ref: _pack_kernel(n, w_ref, o_ref),
        grid=(n_out // tile,),
        in_specs=[pl.BlockSpec((None, k, tile), lambda j: (0, 0, j))],
        out_specs=pl.BlockSpec((k // 2, tile), lambda j: (0, j)),
        out_shape=jax.ShapeDtypeStruct((k // 2, n_out), jnp.uint32),
        name="pack_weight",
    )(w)


def _unpack_rows(w):
    return pltpu.bitcast(w, BF16)


def _sigmoid(x):
    return 1.0 / (1.0 + jnp.exp(-x))


def _silu(x):
    return x * _sigmoid(x)


def _softplus(x):
    return jnp.maximum(x, 0.0) + jnp.log(1.0 + jnp.exp(-jnp.abs(x)))


def _rms(x):
    return x * lax.rsqrt(jnp.mean(x * x, axis=-1, keepdims=True) + NORM_EPS)


def _mod_kernel(c_ref, w_ref, b_ref, o_ref):
    o_ref[...] = _dot(_silu(c_ref[...]), w_ref[...]) + b_ref[...]


def _modulation(c, ada_w, ada_b):
    batch, d = c.shape
    n = ada_w.shape[2]
    return pl.pallas_call(
        _mod_kernel,
        grid=(n // d,),
        in_specs=[
            pl.BlockSpec((batch, d), lambda j: (0, 0)),
            pl.BlockSpec((None, d, d), lambda j: (0, 0, j)),
            pl.BlockSpec((1, d), lambda j: (0, j)),
        ],
        out_specs=pl.BlockSpec((batch, d), lambda j: (0, j)),
        out_shape=jax.ShapeDtypeStruct((batch, n), F32),
        name="modulation",
    )(c, ada_w, ada_b)


def _rope_kernel(cos_ref, sin_ref):
    rows = cos_ref.shape[0]
    pos = (pl.program_id(0) * rows + lax.broadcasted_iota(jnp.int32, (rows, HEAD_DIM), 0)).astype(F32)
    lane = lax.broadcasted_iota(jnp.int32, (rows, HEAD_DIM), 1)
    half = HEAD_DIM // 2
    freq = jnp.where(lane < half, lane, lane - half).astype(F32)
    inv = jnp.exp(freq * (-2.0 * math.log(ROPE_BASE) / HEAD_DIM))
    ang = pos * inv
    cos_ref[...] = jnp.cos(ang)
    sin_ref[...] = jnp.where(lane < half, -jnp.sin(ang), jnp.sin(ang))


def _rope_tables(seq):
    spec = pl.BlockSpec((ROPE_BLOCK, HEAD_DIM), lambda i: (i, 0))
    shape = jax.ShapeDtypeStruct((seq, HEAD_DIM), F32)
    return pl.pallas_call(
        _rope_kernel, grid=(seq // ROPE_BLOCK,), in_specs=[], out_specs=[spec, spec],
        out_shape=[shape, shape], name="rope_tables")()


def _mix_kernel(x_ref, mod_ref, nw_ref, w_ref, cos_ref, sin_ref, cw_ref, alog_ref, dtb_ref,
                rnw_ref, gnw_ref, o_ref,
                ret_a, xpad_a, gz_a, gab_a, ret_b, xpad_b, gz_b, gab_b, rstate_ref, gstate_ref,
                dmask_ref, qdec_ref, kdec_ref):
    tb = ret_a.shape[0]
    n_chunks = tb // CHUNK
    log_gamma = [math.log(1.0 - 2.0 ** (-5.0 - h)) for h in range(HEADS)]
    key_scale = HEAD_DIM ** -0.5
    heads = range(HEADS)
    blocks = (((ret_a, xpad_a, gz_a, gab_a), slice(0, tb)), ((ret_b, xpad_b, gz_b, gab_b), slice(tb, 2 * tb)))

    def head_cols(group, h):
        return slice(group * GROUP_W + h * HEAD_DIM, group * GROUP_W + (h + 1) * HEAD_DIM)

    @pl.when((pl.program_id(0) == 0) & (pl.program_id(1) == 0))
    def _():
        row = lax.broadcasted_iota(jnp.int32, (tb, tb), 0)
        col = lax.broadcasted_iota(jnp.int32, (tb, tb), 1)
        dist = jnp.abs(row - col).astype(F32)
        visible = (col // CHUNK) <= (row // CHUNK)
        ridx = lax.broadcasted_iota(jnp.int32, (tb, HEAD_DIM), 0).astype(F32)
        for h in range(HEADS):
            dmask_ref[h] = jnp.where(visible, jnp.exp(log_gamma[h] * dist) * key_scale, 0.0)
            qdec_ref[h] = jnp.exp(log_gamma[h] * (ridx + 1.0))
            kdec_ref[h] = jnp.exp(log_gamma[h] * (tb - 1.0 - ridx)) * key_scale

    @pl.when(pl.program_id(1) == 0)
    def _():
        rstate_ref[...] = jnp.zeros_like(rstate_ref)
        gstate_ref[...] = jnp.zeros_like(gstate_ref)
        xpad_a[0:CONV_PAD, :] = jnp.zeros((CONV_PAD, CONV_CH), F32)

    shift = mod_ref[0:1, :]
    scale = mod_ref[1:2, :]
    in_gain = nw_ref[...] * (1.0 + scale)

    def project_block(bufs, blk):
        ret_ref, xpad_ref, gz_ref, gab_ref = bufs
        h_in = (_rms(x_ref[blk, :]) * in_gain + shift).astype(BF16)

        def project(dest_ref, row0, col0, width):
            for c in range(0, width, PROJ_TILE):
                n = min(PROJ_TILE, width - c)
                w_tile = _unpack_rows(w_ref[:, col0 + c:col0 + c + n])
                dest_ref[row0:row0 + tb, c:c + n] = jnp.dot(h_in, w_tile, preferred_element_type=F32)

        project(gab_ref, 0, COL_GAB, GAB_W)
        project(xpad_ref, CONV_PAD, COL_CONV, CONV_CH)
        project(ret_ref, 0, COL_RET, COL_CONV - COL_RET)
        project(gz_ref, 0, COL_GZ, GROUP_W)

    def delta_front(bufs):
        _, xpad_ref, _, gab_ref = bufs
        xp = xpad_ref[...]
        conv = xp * cw_ref[0:1, :]
        for w in range(1, CONV_WIDTH):
            conv = xp * cw_ref[w:w + 1, :] + pltpu.roll(conv, 1, 0)
        qkv = _silu(conv[CONV_PAD:, :])

        gab = gab_ref[...]
        g = -jnp.exp(alog_ref[...]) * _softplus(gab + dtb_ref[...])
        beta = _sigmoid(gab)
        in_chunk = lax.broadcasted_iota(jnp.int32, (tb, GAB_W), 0) % CHUNK
        gc = g
        step = 1
        while step < CHUNK:
            gc = gc + jnp.where(in_chunk >= step, pltpu.roll(gc, step, 0), 0.0)
            step *= 2

        f = dict(gq=[], gk=[], gk_b=[], gv=[], gch=[], bh=[], e_gc=[], k_dec=[])
        for h in heads:
            qh = qkv[:, head_cols(0, h)]
            kh = qkv[:, head_cols(1, h)]
            f["gq"].append(qh * lax.rsqrt(jnp.sum(qh * qh, axis=-1, keepdims=True) + NORM_EPS) * (HEAD_DIM ** -0.5))
            gk = kh * lax.rsqrt(jnp.sum(kh * kh, axis=-1, keepdims=True) + NORM_EPS)
            f["gk"].append(gk)
            f["gk_b"].append(gk.astype(BF16))
            f["gv"].append(qkv[:, head_cols(2, h)])
            gch = jnp.broadcast_to(gc[:, h:h + 1], (tb, HEAD_DIM))
            f["gch"].append(gch)
            f["bh"].append(jnp.broadcast_to(beta[:, HEADS + h:HEADS + h + 1], (tb, HEAD_DIM)))
            f["e_gc"].append(jnp.exp(gch))
            g_last = jnp.concatenate(
                [jnp.broadcast_to(gch[(c + 1) * CHUNK - 1:(c + 1) * CHUNK, :], (CHUNK, HEAD_DIM))
                 for c in range(n_chunks)], axis=0)
            f["k_dec"].append(gk * jnp.exp(g_last - gch))
        return f

    def retention(bufs, blk):
        ret_ref = bufs[0]
        cos = cos_ref[blk, :]
        sin = sin_ref[blk, :]
        rq, rk, rvb = [], [], []
        for h in heads:
            q = ret_ref[:, head_cols(0, h)]
            k = ret_ref[:, head_cols(1, h)]
            rq.append(q * cos + pltpu.roll(q, HEAD_DIM // 2, 1) * sin)
            rk.append(k * cos + pltpu.roll(k, HEAD_DIM // 2, 1) * sin)
            rvb.append(ret_ref[:, head_cols(2, h)].astype(BF16))
        scores = [_dot_nt(rq[h], rk[h]) * dmask_ref[h] for h in heads]
        rstates = [rstate_ref[h] for h in heads]
        ro = [_dot(scores[h], rvb[h]) + _dot(rq[h] * qdec_ref[h], rstates[h]) for h in heads]
        for h in heads:
            rstate_ref[h] = rstates[h] * math.exp(log_gamma[h] * tb) + _dot_tn(rk[h] * kdec_ref[h], rvb[h])
        for h in heads:
            mu = jnp.mean(ro[h], axis=-1, keepdims=True)
            d = ro[h] - mu
            var = jnp.mean(d * d, axis=-1, keepdims=True)
            y = d * lax.rsqrt(var + NORM_EPS) * rnw_ref[:, head_cols(0, h)] * _silu(ret_ref[:, head_cols(3, h)])
            o_ref[blk, head_cols(0, h)] = y.astype(o_ref.dtype)

    project_block(*blocks[0])
    xpad_b[0:CONV_PAD, :] = xpad_a[tb:tb + CONV_PAD, :]
    project_block(*blocks[1])
    fronts = [delta_front(bufs) for bufs, _ in blocks]
    for bufs, blk in blocks:
        retention(bufs, blk)
    xpad_a[0:CONV_PAD, :] = xpad_b[tb:tb + CONV_PAD, :]

    pair = 2 * CHUNK
    units = [(i, h, p) for i in range(len(blocks)) for p in range(tb // pair) for h in heads]
    prow = lax.broadcasted_iota(jnp.int32, (pair, pair), 0)
    pcol = lax.broadcasted_iota(jnp.int32, (pair, pair), 1)
    causal = ((prow // CHUNK) == (pcol // CHUNK)) & (prow >= pcol)
    diag = prow == pcol
    eye = jnp.where(diag, 1.0, 0.0)

    def unit_rows(p):
        return slice(p * pair, (p + 1) * pair)

    def operand(name, i, h, p):
        return fronts[i][name][h][unit_rows(p)]

    kk = [_dot_nt(operand("gk_b", *u), operand("gk_b", *u)) for u in units]
    qk = [_dot_nt(operand("gq", *u), operand("gk_b", *u)) for u in units]
    decay, power, inv = [], [], []
    for n, u in enumerate(units):
        gm = operand("gch", *u)
        decay.append(jnp.exp(jnp.where(causal, gm - gm.T, -jnp.inf)))
        power.append(jnp.where(diag, 0.0, -(operand("bh", *u) * kk[n] * decay[n])))
        inv.append(eye + power[n])
    level = 2
    while level < CHUNK:
        power = [_dot(pw, pw) for pw in power]
        inv = [iv + _dot(iv, pw) for iv, pw in zip(inv, power)]
        level *= 2
    uw = []
    for n, u in enumerate(units):
        b = operand("bh", *u)
        rhs = jnp.concatenate([b * operand("gv", *u), b * operand("gk", *u) * operand("e_gc", *u)], axis=1)
        uw.append(_dot(inv[n], rhs))
    a_uw = [_dot(qk[n] * decay[n], uw[n]) for n in range(len(units))]
    q_eff, k_uw = [], []
    for n, u in enumerate(units):
        q_eff.append(operand("gq", *u) * operand("e_gc", *u) - a_uw[n][:, HEAD_DIM:])
        kd_t = operand("k_dec", *u).T
        k_uw.append([_dot(jnp.where(pcol // CHUNK == c, kd_t, 0.0), uw[n])
                     for c in range(pair // CHUNK)])
    gstates = [gstate_ref[h] for h in heads]
    for i, (bufs, blk) in enumerate(blocks):
        outs = [[] for _ in heads]
        for c in range(n_chunks):
            p, cl = divmod(c, pair // CHUNK)
            for h in heads:
                n = units.index((i, h, p))
                local = slice(cl * CHUNK, (cl + 1) * CHUNK)
                sb = gstates[h].astype(BF16)
                outs[h].append(_dot(q_eff[n][local], sb) + a_uw[n][local, :HEAD_DIM])
                gch = fronts[i]["gch"][h]
                chunk_decay = jnp.exp(gch[(c + 1) * CHUNK - 1:(c + 1) * CHUNK, :])
                gstates[h] = (gstates[h] * chunk_decay - _dot(k_uw[n][cl][:, HEAD_DIM:], sb)
                              + k_uw[n][cl][:, :HEAD_DIM])
        gz_ref = bufs[2]
        for h in heads:
            o = jnp.concatenate(outs[h], axis=0)
            y = _rms(o) * gnw_ref[...] * _silu(gz_ref[:, head_cols(0, h)])
            o_ref[blk, head_cols(1, h)] = y.astype(o_ref.dtype)
    for h in heads:
        gstate_ref[h] = gstates[h]


def _token_mix(x, mod, norm_w, w_proj, cos_t, sin_t, conv_w, a_log_pad, dt_pad, ret_norm_w, gdn_norm_w):
    batch, seq, d = x.shape
    tb = TIME_BLOCK
    step_rows = 2 * tb
    const = lambda shape: pl.BlockSpec(shape, lambda b, t: (0,) * len(shape))
    proj_bufs = [
        pltpu.VMEM((tb, 4 * GROUP_W), F32),
        pltpu.VMEM((tb + CONV_PAD, CONV_CH), F32),
        pltpu.VMEM((tb, GROUP_W), F32),
        pltpu.VMEM((tb, GAB_W), F32),
    ]
    return pl.pallas_call(
        _mix_kernel,
        grid=(batch, seq // step_rows),
        in_specs=[
            pl.BlockSpec((None, step_rows, d), lambda b, t: (b, t, 0)),
            pl.BlockSpec((None, N_MOD, d), lambda b, t: (b, 0, 0)),
            const((1, d)),
            pl.BlockSpec((d // 2, PROJ_W), lambda b, t: (0, 0), pipeline_mode=pl.Buffered(1)),
            pl.BlockSpec((step_rows, HEAD_DIM), lambda b, t: (t, 0)),
            pl.BlockSpec((step_rows, HEAD_DIM), lambda b, t: (t, 0)),
            const((CONV_WIDTH, CONV_CH)),
            const((1, GAB_W)),
            const((1, GAB_W)),
            const((1, GROUP_W)),
            const((1, HEAD_DIM)),
        ],
        out_specs=pl.BlockSpec((None, step_rows, 2 * GROUP_W), lambda b, t: (b, t, 0)),
        out_shape=jax.ShapeDtypeStruct((batch, seq, 2 * GROUP_W), BF16),
        scratch_shapes=proj_bufs + proj_bufs + [
            pltpu.VMEM((HEADS, HEAD_DIM, HEAD_DIM), F32),
            pltpu.VMEM((HEADS, HEAD_DIM, HEAD_DIM), F32),
            pltpu.VMEM((HEADS, tb, tb), F32),
            pltpu.VMEM((HEADS, tb, HEAD_DIM), F32),
            pltpu.VMEM((HEADS, tb, HEAD_DIM), F32),
        ],
        compiler_params=pltpu.CompilerParams(
            dimension_semantics=("arbitrary", "arbitrary"), vmem_limit_bytes=VMEM_LIMIT_BYTES),
        name="token_mix",
    )(x, mod, norm_w, w_proj, cos_t, sin_t, conv_w, a_log_pad, dt_pad, ret_norm_w, gdn_norm_w)


def _channel_kernel(x_ref, mixed_ref, mod_ref, wo_ref, nw_ref, w1_ref, w2_ref, fw_ref, o_ref):
    gate_a = mod_ref[2:3, :]
    shift = mod_ref[3:4, :]
    scale = mod_ref[4:5, :]
    gate_m = mod_ref[5:6, :]
    x1 = x_ref[...] + gate_a * jnp.dot(mixed_ref[...], _unpack_rows(wo_ref[...]), preferred_element_type=F32)
    h = (_rms(x1) * nw_ref[...] * (1.0 + scale) + shift).astype(BF16)
    d_ff = w1_ref.shape[1]
    acc = jnp.zeros(x1.shape, F32)
    for j in range(d_ff // FF_BLOCK):
        cols = slice(j * FF_BLOCK, (j + 1) * FF_BLOCK)
        a = jnp.maximum(jnp.dot(h, _unpack_rows(w1_ref[:, cols]), preferred_element_type=F32), 0.0)
        w2_rows = _unpack_rows(w2_ref[j * FF_BLOCK // 2:(j + 1) * FF_BLOCK // 2, :])
        acc = acc + jnp.dot((a * a).astype(BF16), w2_rows, preferred_element_type=F32)
    x2 = x1 + gate_m * acc
    o_ref[...] = _rms(x2) * fw_ref[...]


def _channel_mix(x, mixed, mod, w_out, norm_w, w_ff1, w_ff2, final_w):
    batch, seq, d = x.shape
    d_ff = w_ff1.shape[1]
    rb = ROW_BLOCK
    per_batch = seq // rb
    resident = lambda shape: pl.BlockSpec(shape, lambda i: (0, 0), pipeline_mode=pl.Buffered(1))
    out = pl.pallas_call(
        _channel_kernel,
        grid=(batch * per_batch,),
        in_specs=[
            pl.BlockSpec((rb, d), lambda i: (i, 0)),
            pl.BlockSpec((rb, mixed.shape[-1]), lambda i: (i, 0)),
            pl.BlockSpec((None, N_MOD, d), lambda i: (i // per_batch, 0, 0)),
            resident((mixed.shape[-1] // 2, d)),
            pl.BlockSpec((1, d), lambda i: (0, 0)),
            resident((d // 2, d_ff)),
            resident((d_ff // 2, d)),
            pl.BlockSpec((1, d), lambda i: (0, 0)),
        ],
        out_specs=pl.BlockSpec((rb, d), lambda i: (i, 0)),
        out_shape=jax.ShapeDtypeStruct((batch * seq, d), F32),
        compiler_params=pltpu.CompilerParams(
            dimension_semantics=("arbitrary",), vmem_limit_bytes=VMEM_LIMIT_BYTES),
        name="channel_mix",
    )(x.reshape(batch * seq, d), mixed.reshape(batch * seq, -1), mod, w_out, norm_w, w_ff1, w_ff2, final_w)
    return out.reshape(batch, seq, d)


def _pad_lanes(v, width):
    return jnp.pad(v, (0, width - v.shape[0])).reshape(1, width)


def kernel(x, c, ada_w, ada_b, norm_mix_w, w_in, conv_w, a_log, dt_bias, ret_norm_w, gdn_norm_w, w_out,
           norm_mlp_w, w_ff1, w_ff2, norm_final_w):
    batch, seq, d = x.shape
    assert ada_w.shape[0] == 1, "single-layer block: the final rmsnorm is fused into the channel-mix call"
    cos_t, sin_t = _rope_tables(seq)
    mod = _modulation(c, ada_w, ada_b).reshape(batch, N_MOD, d)
    mixed = _token_mix(
        x, mod, norm_mix_w[0].reshape(1, d), _pack_rows(w_in, PROJ_W), cos_t, sin_t, conv_w[0],
        _pad_lanes(a_log[0], GAB_W), _pad_lanes(dt_bias[0], GAB_W),
        ret_norm_w[0].reshape(1, GROUP_W), gdn_norm_w[0].reshape(1, HEAD_DIM))
    return _channel_mix(x, mixed, mod, _pack_rows(w_out), norm_mlp_w[0].reshape(1, d),
                        _pack_rows(w_ff1), _pack_rows(w_ff2), norm_final_w.reshape(1, d))
```

```python
import math

import jax
import jax.numpy as jnp
from jax import lax
from jax.experimental import pallas as pl
from jax.experimental.pallas import tpu as pltpu

CHUNK = 64
HEADS = 4
HEAD_DIM = 128
GROUP_W = HEADS * HEAD_DIM
CONV_WIDTH = 4
CONV_CH = 3 * GROUP_W
CONV_PAD = 8
ROPE_BASE = 10000.0
NORM_EPS = 1e-6
N_MOD = 6

TIME_BLOCK = 256
PROJ_TILE = 256
ROW_BLOCK = 512
FF_BLOCK = 1024
ROPE_BLOCK = 512
LANES = 128
PACK_BLOCK_BYTES = 6 * 1024 * 1024
VMEM_LIMIT_BYTES = 48 * 1024 * 1024

COL_RET = 0
COL_CONV = 4 * GROUP_W
COL_GZ = COL_CONV + CONV_CH
COL_GAB = COL_GZ + GROUP_W
GAB_W = LANES
PROJ_W = COL_GAB + GAB_W

NT_DIMS = (((1,), (1,)), ((), ()))
TN_DIMS = (((0,), (0,)), ((), ()))

BF16 = jnp.bfloat16
F32 = jnp.float32


def _dot(a, b):
    return jnp.dot(a.astype(BF16), b.astype(BF16), preferred_element_type=F32)


def _dot_nt(a, b):
    return lax.dot_general(a.astype(BF16), b.astype(BF16), NT_DIMS, preferred_element_type=F32)


def _dot_tn(a, b):
    return lax.dot_general(a.astype(BF16), b.astype(BF16), TN_DIMS, preferred_element_type=F32)


def _pack_kernel(n_valid, col_axis, w_ref, o_ref):
    w = w_ref[...]
    tile = w.shape[col_axis]
    if n_valid % tile:
        col = pl.program_id(0) * tile + lax.broadcasted_iota(jnp.int32, w.shape, col_axis)
        w = jnp.where(col < n_valid, w, 0.0)
    if col_axis == 0:
        w = w.T
    o_ref[...] = pltpu.bitcast(w.astype(BF16), jnp.uint32)


def _pack_rows(w, n_out=None, transposed=False):
    k, n = (w.shape[2], w.shape[1]) if transposed else (w.shape[1], w.shape[2])
    n_out = n if n_out is None else n_out
    tile = max(t for t in range(LANES, n_out + 1, LANES)
               if n_out % t == 0 and (k * t * 4 <= PACK_BLOCK_BYTES or t == LANES))
    in_spec = (pl.BlockSpec((None, tile, k), lambda j: (0, j, 0)) if transposed
               else pl.BlockSpec((None, k, tile), lambda j: (0, 0, j)))
    return pl.pallas_call(
        lambda w_ref, o_ref: _pack_kernel(n, 0 if transposed else 1, w_ref, o_ref),
        grid=(n_out // tile,),
        in_specs=[in_spec],
        out_specs=pl.BlockSpec((k // 2, tile), lambda j: (0, j)),
        out_shape=jax.ShapeDtypeStruct((k // 2, n_out), jnp.uint32),
        name="pack_weight",
    )(w)


def _unpack_rows(w):
    return pltpu.bitcast(w, BF16)


def _sigmoid(x):
    return 1.0 / (1.0 + jnp.exp(-x))


def _silu(x):
    return x * _sigmoid(x)


def _softplus(x):
    return jnp.maximum(x, 0.0) + jnp.log(1.0 + jnp.exp(-jnp.abs(x)))


def _rms(x):
    return x * lax.rsqrt(jnp.mean(x * x, axis=-1, keepdims=True) + NORM_EPS)


def _mod_kernel(c_ref, w_ref, b_ref, o_ref):
    o_ref[...] = _dot(_silu(c_ref[...]), w_ref[...]) + b_ref[...]


def _modulation(c, ada_w, ada_b):
    batch, d = c.shape
    n = ada_w.shape[2]
    return pl.pallas_call(
        _mod_kernel,
        grid=(n // d,),
        in_specs=[
            pl.BlockSpec((batch, d), lambda j: (0, 0)),
            pl.BlockSpec((None, d, d), lambda j: (0, 0, j)),
            pl.BlockSpec((1, d), lambda j: (0, j)),
        ],
        out_specs=pl.BlockSpec((batch, d), lambda j: (0, j)),
        out_shape=jax.ShapeDtypeStruct((batch, n), F32),
        name="modulation",
    )(c, ada_w, ada_b)


def _rope_kernel(cos_ref, sin_ref):
    rows = cos_ref.shape[0]
    pos = (pl.program_id(0) * rows + lax.broadcasted_iota(jnp.int32, (rows, HEAD_DIM), 0)).astype(F32)
    lane = lax.broadcasted_iota(jnp.int32, (rows, HEAD_DIM), 1)
    half = HEAD_DIM // 2
    freq = jnp.where(lane < half, lane, lane - half).astype(F32)
    inv = jnp.exp(freq * (-2.0 * math.log(ROPE_BASE) / HEAD_DIM))
    ang = pos * inv
    cos_ref[...] = jnp.cos(ang)
    sin_ref[...] = jnp.where(lane < half, -jnp.sin(ang), jnp.sin(ang))


def _rope_tables(seq):
    spec = pl.BlockSpec((ROPE_BLOCK, HEAD_DIM), lambda i: (i, 0))
    shape = jax.ShapeDtypeStruct((seq, HEAD_DIM), F32)
    return pl.pallas_call(
        _rope_kernel, grid=(seq // ROPE_BLOCK,), in_specs=[], out_specs=[spec, spec],
        out_shape=[shape, shape], name="rope_tables")()


def _mix_kernel(x_ref, mod_ref, nw_ref, w_ref, cos_ref, sin_ref, cw_ref, alog_ref, dtb_ref,
                rnw_ref, gnw_ref, o_ref,
                ret_a, xpad_a, gz_a, gab_a, ret_b, xpad_b, gz_b, gab_b, rstate_ref, gstate_ref,
                dmask_ref, qdec_ref, kdec_ref):
    tb = ret_a.shape[0]
    n_chunks = tb // CHUNK
    log_gamma = [math.log(1.0 - 2.0 ** (-5.0 - h)) for h in range(HEADS)]
    key_scale = HEAD_DIM ** -0.5
    heads = range(HEADS)
    blocks = (((ret_a, xpad_a, gz_a, gab_a), slice(0, tb)), ((ret_b, xpad_b, gz_b, gab_b), slice(tb, 2 * tb)))

    def head_cols(group, h):
        return slice(group * GROUP_W + h * HEAD_DIM, group * GROUP_W + (h + 1) * HEAD_DIM)

    @pl.when((pl.program_id(0) == 0) & (pl.program_id(1) == 0))
    def _():
        row = lax.broadcasted_iota(jnp.int32, (tb, tb), 0)
        col = lax.broadcasted_iota(jnp.int32, (tb, tb), 1)
        dist = jnp.abs(row - col).astype(F32)
        visible = (col // CHUNK) <= (row // CHUNK)
        ridx = lax.broadcasted_iota(jnp.int32, (tb, HEAD_DIM), 0).astype(F32)
        for h in range(HEADS):
            dmask_ref[h] = jnp.where(visible, jnp.exp(log_gamma[h] * dist) * key_scale, 0.0)
            qdec_ref[h] = jnp.exp(log_gamma[h] * (ridx + 1.0))
            kdec_ref[h] = jnp.exp(log_gamma[h] * (tb - 1.0 - ridx)) * key_scale

    @pl.when(pl.program_id(1) == 0)
    def _():
        rstate_ref[...] = jnp.zeros_like(rstate_ref)
        gstate_ref[...] = jnp.zeros_like(gstate_ref)
        xpad_a[0:CONV_PAD, :] = jnp.zeros((CONV_PAD, CONV_CH), F32)

    shift = mod_ref[0:1, :]
    scale = mod_ref[1:2, :]
    in_gain = nw_ref[...] * (1.0 + scale)

    def project_block(bufs, blk):
        ret_ref, xpad_ref, gz_ref, gab_ref = bufs
        h_in = (_rms(x_ref[blk, :]) * in_gain + shift).astype(BF16)

        def project(dest_ref, row0, col0, width):
            for c in range(0, width, PROJ_TILE):
                n = min(PROJ_TILE, width - c)
                w_tile = _unpack_rows(w_ref[:, col0 + c:col0 + c + n])
                dest_ref[row0:row0 + tb, c:c + n] = jnp.dot(h_in, w_tile, preferred_element_type=F32)

        project(gab_ref, 0, COL_GAB, GAB_W)
        project(xpad_ref, CONV_PAD, COL_CONV, CONV_CH)
        project(ret_ref, 0, COL_RET, COL_CONV - COL_RET)
        project(gz_ref, 0, COL_GZ, GROUP_W)

    def delta_front(bufs):
        _, xpad_ref, _, gab_ref = bufs
        xp = xpad_ref[...]
        conv = xp * cw_ref[0:1, :]
        for w in range(1, CONV_WIDTH):
            conv = xp * cw_ref[w:w + 1, :] + pltpu.roll(conv, 1, 0)
        qkv = _silu(conv[CONV_PAD:, :])

        gab = gab_ref[...]
        g = -jnp.exp(alog_ref[...]) * _softplus(gab + dtb_ref[...])
        beta = _sigmoid(gab)
        in_chunk = lax.broadcasted_iota(jnp.int32, (tb, GAB_W), 0) % CHUNK
        gc = g
        step = 1
        while step < CHUNK:
            gc = gc + jnp.where(in_chunk >= step, pltpu.roll(gc, step, 0), 0.0)
            step *= 2

        f = dict(gq=[], gk=[], gk_b=[], gv=[], gch=[], bh=[], e_gc=[], k_dec=[])
        for h in heads:
            qh = qkv[:, head_cols(0, h)]
            kh = qkv[:, head_cols(1, h)]
            f["gq"].append(qh * lax.rsqrt(jnp.sum(qh * qh, axis=-1, keepdims=True) + NORM_EPS) * (HEAD_DIM ** -0.5))
            gk = kh * lax.rsqrt(jnp.sum(kh * kh, axis=-1, keepdims=True) + NORM_EPS)
            f["gk"].append(gk)
            f["gk_b"].append(gk.astype(BF16))
            f["gv"].append(qkv[:, head_cols(2, h)])
            gch = jnp.broadcast_to(gc[:, h:h + 1], (tb, HEAD_DIM))
            f["gch"].append(gch)
            f["bh"].append(jnp.broadcast_to(beta[:, HEADS + h:HEADS + h + 1], (tb, HEAD_DIM)))
            f["e_gc"].append(jnp.exp(gch))
            g_last = jnp.concatenate(
                [jnp.broadcast_to(gch[(c + 1) * CHUNK - 1:(c + 1) * CHUNK, :], (CHUNK, HEAD_DIM))
                 for c in range(n_chunks)], axis=0)
            f["k_dec"].append(gk * jnp.exp(g_last - gch))
        return f

    def retention(bufs, blk):
        ret_ref = bufs[0]
        cos = cos_ref[blk, :]
        sin = sin_ref[blk, :]
        rq, rk, rvb = [], [], []
        for h in heads:
            q = ret_ref[:, head_cols(0, h)]
            k = ret_ref[:, head_cols(1, h)]
            rq.append(q * cos + pltpu.roll(q, HEAD_DIM // 2, 1) * sin)
            rk.append(k * cos + pltpu.roll(k, HEAD_DIM // 2, 1) * sin)
            rvb.append(ret_ref[:, head_cols(2, h)].astype(BF16))
        scores = [_dot_nt(rq[h], rk[h]) * dmask_ref[h] for h in heads]
        rstates = [rstate_ref[h] for h in heads]
        ro = [_dot(scores[h], rvb[h]) + _dot(rq[h] * qdec_ref[h], rstates[h]) for h in heads]
        for h in heads:
            rstate_ref[h] = rstates[h] * math.exp(log_gamma[h] * tb) + _dot_tn(rk[h] * kdec_ref[h], rvb[h])
        for h in heads:
            mu = jnp.mean(ro[h], axis=-1, keepdims=True)
            d = ro[h] - mu
            var = jnp.mean(d * d, axis=-1, keepdims=True)
            y = d * lax.rsqrt(var + NORM_EPS) * rnw_ref[:, head_cols(0, h)] * _silu(ret_ref[:, head_cols(3, h)])
            o_ref[blk, head_cols(0, h)] = y.astype(o_ref.dtype)

    project_block(*blocks[0])
    xpad_b[0:CONV_PAD, :] = xpad_a[tb:tb + CONV_PAD, :]
    project_block(*blocks[1])
    fronts = [delta_front(bufs) for bufs, _ in blocks]
    for bufs, blk in blocks:
        retention(bufs, blk)
    xpad_a[0:CONV_PAD, :] = xpad_b[tb:tb + CONV_PAD, :]

    pair = 2 * CHUNK
    units = [(i, h, p) for i in range(len(blocks)) for p in range(tb // pair) for h in heads]
    prow = lax.broadcasted_iota(jnp.int32, (pair, pair), 0)
    pcol = lax.broadcasted_iota(jnp.int32, (pair, pair), 1)
    causal = ((prow // CHUNK) == (pcol // CHUNK)) & (prow >= pcol)
    diag = prow == pcol
    eye = jnp.where(diag, 1.0, 0.0)

    def unit_rows(p):
        return slice(p * pair, (p + 1) * pair)

    def operand(name, i, h, p):
        return fronts[i][name][h][unit_rows(p)]

    kk = [_dot_nt(operand("gk_b", *u), operand("gk_b", *u)) for u in units]
    qk = [_dot_nt(operand("gq", *u), operand("gk_b", *u)) for u in units]
    decay, power, inv = [], [], []
    for n, u in enumerate(units):
        gm = operand("gch", *u)
        decay.append(jnp.exp(jnp.where(causal, gm - gm.T, -jnp.inf)))
        power.append(jnp.where(diag, 0.0, -(operand("bh", *u) * kk[n] * decay[n])))
        inv.append(eye + power[n])
    level = 2
    while level < CHUNK:
        power = [_dot(pw, pw) for pw in power]
        inv = [iv + _dot(iv, pw) for iv, pw in zip(inv, power)]
        level *= 2
    uw = []
    for n, u in enumerate(units):
        b = operand("bh", *u)
        rhs = jnp.concatenate([b * operand("gv", *u), b * operand("gk", *u) * operand("e_gc", *u)], axis=1)
        uw.append(_dot(inv[n], rhs))
    a_uw = [_dot(qk[n] * decay[n], uw[n]) for n in range(len(units))]
    q_eff, k_uw = [], []
    for n, u in enumerate(units):
        q_eff.append(operand("gq", *u) * operand("e_gc", *u) - a_uw[n][:, HEAD_DIM:])
        kd_t = operand("k_dec", *u).T
        k_uw.append([_dot(jnp.where(pcol // CHUNK == c, kd_t, 0.0), uw[n])
                     for c in range(pair // CHUNK)])
    gstates = [gstate_ref[h] for h in heads]
    for i, (bufs, blk) in enumerate(blocks):
        outs = [[] for _ in heads]
        for c in range(n_chunks):
            p, cl = divmod(c, pair // CHUNK)
            for h in heads:
                n = units.index((i, h, p))
                local = slice(cl * CHUNK, (cl + 1) * CHUNK)
                sb = gstates[h].astype(BF16)
                outs[h].append(_dot(q_eff[n][local], sb) + a_uw[n][local, :HEAD_DIM])
                gch = fronts[i]["gch"][h]
                chunk_decay = jnp.exp(gch[(c + 1) * CHUNK - 1:(c + 1) * CHUNK, :])
                gstates[h] = (gstates[h] * chunk_decay - _dot(k_uw[n][cl][:, HEAD_DIM:], sb)
                              + k_uw[n][cl][:, :HEAD_DIM])
        gz_ref = bufs[2]
        for h in heads:
            o = jnp.concatenate(outs[h], axis=0)
            y = _rms(o) * gnw_ref[...] * _silu(gz_ref[:, head_cols(0, h)])
            o_ref[blk, head_cols(1, h)] = y.astype(o_ref.dtype)
    for h in heads:
        gstate_ref[h] = gstates[h]


def _token_mix(x, mod, norm_w, w_proj, cos_t, sin_t, conv_w, a_log_pad, dt_pad, ret_norm_w, gdn_norm_w):
    batch, seq, d = x.shape
    tb = TIME_BLOCK
    step_rows = 2 * tb
    const = lambda shape: pl.BlockSpec(shape, lambda b, t: (0,) * len(shape))
    proj_bufs = [
        pltpu.VMEM((tb, 4 * GROUP_W), F32),
        pltpu.VMEM((tb + CONV_PAD, CONV_CH), F32),
        pltpu.VMEM((tb, GROUP_W), F32),
        pltpu.VMEM((tb, GAB_W), F32),
    ]
    return pl.pallas_call(
        _mix_kernel,
        grid=(batch, seq // step_rows),
        in_specs=[
            pl.BlockSpec((None, step_rows, d), lambda b, t: (b, t, 0)),
            pl.BlockSpec((None, N_MOD, d), lambda b, t: (b, 0, 0)),
            const((1, d)),
            pl.BlockSpec((d // 2, PROJ_W), lambda b, t: (0, 0), pipeline_mode=pl.Buffered(1)),
            pl.BlockSpec((step_rows, HEAD_DIM), lambda b, t: (t, 0)),
            pl.BlockSpec((step_rows, HEAD_DIM), lambda b, t: (t, 0)),
            const((CONV_WIDTH, CONV_CH)),
            const((1, GAB_W)),
            const((1, GAB_W)),
            const((1, GROUP_W)),
            const((1, HEAD_DIM)),
        ],
        out_specs=pl.BlockSpec((None, step_rows, 2 * GROUP_W), lambda b, t: (b, t, 0)),
        out_shape=jax.ShapeDtypeStruct((batch, seq, 2 * GROUP_W), BF16),
        scratch_shapes=proj_bufs + proj_bufs + [
            pltpu.VMEM((HEADS, HEAD_DIM, HEAD_DIM), F32),
            pltpu.VMEM((HEADS, HEAD_DIM, HEAD_DIM), F32),
            pltpu.VMEM((HEADS, tb, tb), F32),
            pltpu.VMEM((HEADS, tb, HEAD_DIM), F32),
            pltpu.VMEM((HEADS, tb, HEAD_DIM), F32),
        ],
        compiler_params=pltpu.CompilerParams(
            dimension_semantics=("arbitrary", "arbitrary"), vmem_limit_bytes=VMEM_LIMIT_BYTES),
        name="token_mix",
    )(x, mod, norm_w, w_proj, cos_t, sin_t, conv_w, a_log_pad, dt_pad, ret_norm_w, gdn_norm_w)


def _channel_kernel(x_ref, mixed_ref, mod_ref, wo_ref, nw_ref, w1_ref, w2_ref, fw_ref, o_ref):
    gate_a = mod_ref[2:3, :]
    shift = mod_ref[3:4, :]
    scale = mod_ref[4:5, :]
    gate_m = mod_ref[5:6, :]
    x1 = x_ref[...] + gate_a * jnp.dot(mixed_ref[...], _unpack_rows(wo_ref[...]), preferred_element_type=F32)
    h = (_rms(x1) * nw_ref[...] * (1.0 + scale) + shift).astype(BF16)
    d_ff = w1_ref.shape[1]
    acc = jnp.zeros(x1.shape, F32)
    for j in range(d_ff // FF_BLOCK):
        cols = slice(j * FF_BLOCK, (j + 1) * FF_BLOCK)
        a = jnp.maximum(jnp.dot(h, _unpack_rows(w1_ref[:, cols]), preferred_element_type=F32), 0.0)
        w2_rows = _unpack_rows(w2_ref[j * FF_BLOCK // 2:(j + 1) * FF_BLOCK // 2, :])
        acc = acc + jnp.dot((a * a).astype(BF16), w2_rows, preferred_element_type=F32)
    x2 = x1 + gate_m * acc
    o_ref[...] = _rms(x2) * fw_ref[...]


def _channel_mix(x, mixed, mod, w_out, norm_w, w_ff1, w_ff2, final_w):
    batch, seq, d = x.shape
    d_ff = w_ff1.shape[1]
    rb = ROW_BLOCK
    per_batch = seq // rb
    resident = lambda shape: pl.BlockSpec(shape, lambda i: (0, 0), pipeline_mode=pl.Buffered(1))
    out = pl.pallas_call(
        _channel_kernel,
        grid=(batch * per_batch,),
        in_specs=[
            pl.BlockSpec((rb, d), lambda i: (i, 0)),
            pl.BlockSpec((rb, mixed.shape[-1]), lambda i: (i, 0)),
            pl.BlockSpec((None, N_MOD, d), lambda i: (i // per_batch, 0, 0)),
            resident((mixed.shape[-1] // 2, d)),
            pl.BlockSpec((1, d), lambda i: (0, 0)),
            resident((d // 2, d_ff)),
            resident((d_ff // 2, d)),
            pl.BlockSpec((1, d), lambda i: (0, 0)),
        ],
        out_specs=pl.BlockSpec((rb, d), lambda i: (i, 0)),
        out_shape=jax.ShapeDtypeStruct((batch * seq, d), F32),
        compiler_params=pltpu.CompilerParams(
            dimension_semantics=("arbitrary",), vmem_limit_bytes=VMEM_LIMIT_BYTES),
        name="channel_mix",
    )(x.reshape(batch * seq, d), mixed.reshape(batch * seq, -1), mod, w_out, norm_w, w_ff1, w_ff2, final_w)
    return out.reshape(batch, seq, d)


def _pad_lanes(v, width):
    return jnp.pad(v, (0, width - v.shape[0])).reshape(1, width)


def kernel(x, c, ada_w, ada_b, norm_mix_w, w_in, conv_w, a_log, dt_bias, ret_norm_w, gdn_norm_w, w_out,
           norm_mlp_w, w_ff1, w_ff2, norm_final_w):
    batch, seq, d = x.shape
    assert ada_w.shape[0] == 1, "single-layer block: the final rmsnorm is fused into the channel-mix call"
    cos_t, sin_t = _rope_tables(seq)
    mod = _modulation(c, ada_w, ada_b).reshape(batch, N_MOD, d)
    w_proj = _pack_rows(jnp.swapaxes(w_in, 1, 2), PROJ_W, transposed=True)
    mixed = _token_mix(
        x, mod, norm_mix_w[0].reshape(1, d), w_proj, cos_t, sin_t, conv_w[0],
        _pad_lanes(a_log[0], GAB_W), _pad_lanes(dt_bias[0], GAB_W),
        ret_norm_w[0].reshape(1, GROUP_W), gdn_norm_w[0].reshape(1, HEAD_DIM))
    return _channel_mix(x, mixed, mod, _pack_rows(w_out), norm_mlp_w[0].reshape(1, d),
                        _pack_rows(w_ff1), _pack_rows(w_ff2), norm_final_w.reshape(1, d))
```

```python
import math

import jax
import jax.numpy as jnp
from jax import lax
from jax.experimental import pallas as pl
from jax.experimental.pallas import tpu as pltpu

CHUNK = 64
HEADS = 4
HEAD_DIM = 128
GROUP_W = HEADS * HEAD_DIM
CONV_WIDTH = 4
CONV_CH = 3 * GROUP_W
CONV_PAD = 8
ROPE_BASE = 10000.0
NORM_EPS = 1e-6
N_MOD = 6

TIME_BLOCK = 256
PROJ_TILE = 256
ROW_BLOCK = 512
FF_BLOCK = 1024
ROPE_BLOCK = 512
LANES = 128
PACK_BLOCK_BYTES = 6 * 1024 * 1024
VMEM_LIMIT_BYTES = 48 * 1024 * 1024

COL_RET = 0
COL_CONV = 4 * GROUP_W
COL_GZ = COL_CONV + CONV_CH
COL_GAB = COL_GZ + GROUP_W
GAB_W = LANES
PROJ_W = COL_GAB + GAB_W

NT_DIMS = (((1,), (1,)), ((), ()))
TN_DIMS = (((0,), (0,)), ((), ()))

BF16 = jnp.bfloat16
F32 = jnp.float32


def _dot(a, b):
    return jnp.dot(a.astype(BF16), b.astype(BF16), preferred_element_type=F32)


def _dot_nt(a, b):
    return lax.dot_general(a.astype(BF16), b.astype(BF16), NT_DIMS, preferred_element_type=F32)


def _dot_tn(a, b):
    return lax.dot_general(a.astype(BF16), b.astype(BF16), TN_DIMS, preferred_element_type=F32)


def _pack_kernel(n_valid, col_axis, w_ref, o_ref):
    w = w_ref[...]
    tile = w.shape[col_axis]
    if n_valid % tile:
        col = pl.program_id(0) * tile + lax.broadcasted_iota(jnp.int32, w.shape, col_axis)
        w = jnp.where(col < n_valid, w, 0.0)
    if col_axis == 0:
        w = w.T
    o_ref[...] = pltpu.bitcast(w.astype(BF16), jnp.uint32)


def _pack_rows(w, n_out=None, transposed=False):
    k, n = (w.shape[2], w.shape[1]) if transposed else (w.shape[1], w.shape[2])
    n_out = n if n_out is None else n_out
    tile = max(t for t in range(LANES, n_out + 1, LANES)
               if n_out % t == 0 and (k * t * 4 <= PACK_BLOCK_BYTES or t == LANES))
    in_spec = (pl.BlockSpec((None, tile, k), lambda j: (0, j, 0)) if transposed
               else pl.BlockSpec((None, k, tile), lambda j: (0, 0, j)))
    return pl.pallas_call(
        lambda w_ref, o_ref: _pack_kernel(n, 0 if transposed else 1, w_ref, o_ref),
        grid=(n_out // tile,),
        in_specs=[in_spec],
        out_specs=pl.BlockSpec((k // 2, tile), lambda j: (0, j)),
        out_shape=jax.ShapeDtypeStruct((k // 2, n_out), jnp.uint32),
        name="pack_weight",
    )(w)


def _unpack_rows(w):
    return pltpu.bitcast(w, BF16)


def _sigmoid(x):
    return 1.0 / (1.0 + jnp.exp(-x))


def _silu(x):
    return x * _sigmoid(x)


def _softplus(x):
    return jnp.maximum(x, 0.0) + jnp.log(1.0 + jnp.exp(-jnp.abs(x)))


def _rms(x):
    return x * lax.rsqrt(jnp.mean(x * x, axis=-1, keepdims=True) + NORM_EPS)


def _mod_kernel(c_ref, w_ref, b_ref, o_ref):
    o_ref[...] = _dot(_silu(c_ref[...]), w_ref[...]) + b_ref[...]


def _modulation(c, ada_w, ada_b):
    batch, d = c.shape
    n = ada_w.shape[2]
    return pl.pallas_call(
        _mod_kernel,
        grid=(n // d,),
        in_specs=[
            pl.BlockSpec((batch, d), lambda j: (0, 0)),
            pl.BlockSpec((None, d, d), lambda j: (0, 0, j)),
            pl.BlockSpec((1, d), lambda j: (0, j)),
        ],
        out_specs=pl.BlockSpec((batch, d), lambda j: (0, j)),
        out_shape=jax.ShapeDtypeStruct((batch, n), F32),
        name="modulation",
    )(c, ada_w, ada_b)


def _rope_kernel(cos_ref, sin_ref):
    rows = cos_ref.shape[0]
    pos = (pl.program_id(0) * rows + lax.broadcasted_iota(jnp.int32, (rows, HEAD_DIM), 0)).astype(F32)
    lane = lax.broadcasted_iota(jnp.int32, (rows, HEAD_DIM), 1)
    half = HEAD_DIM // 2
    freq = jnp.where(lane < half, lane, lane - half).astype(F32)
    inv = jnp.exp(freq * (-2.0 * math.log(ROPE_BASE) / HEAD_DIM))
    ang = pos * inv
    cos_ref[...] = jnp.cos(ang)
    sin_ref[...] = jnp.where(lane < half, -jnp.sin(ang), jnp.sin(ang))


def _rope_tables(seq):
    spec = pl.BlockSpec((ROPE_BLOCK, HEAD_DIM), lambda i: (i, 0))
    shape = jax.ShapeDtypeStruct((seq, HEAD_DIM), F32)
    return pl.pallas_call(
        _rope_kernel, grid=(seq // ROPE_BLOCK,), in_specs=[], out_specs=[spec, spec],
        out_shape=[shape, shape], name="rope_tables")()


def _mix_kernel(x_ref, mod_ref, nw_ref, w_ref, cos_ref, sin_ref, cw_ref, alog_ref, dtb_ref,
                rnw_ref, gnw_ref, o_ref,
                ret_a, xpad_a, gz_a, gab_a, ret_b, xpad_b, gz_b, gab_b, rstate_ref, gstate_ref,
                dmask_ref, qdec_ref, kdec_ref):
    tb = ret_a.shape[0]
    n_chunks = tb // CHUNK
    log_gamma = [math.log(1.0 - 2.0 ** (-5.0 - h)) for h in range(HEADS)]
    key_scale = HEAD_DIM ** -0.5
    heads = range(HEADS)
    blocks = (((ret_a, xpad_a, gz_a, gab_a), slice(0, tb)), ((ret_b, xpad_b, gz_b, gab_b), slice(tb, 2 * tb)))

    def head_cols(group, h):
        return slice(group * GROUP_W + h * HEAD_DIM, group * GROUP_W + (h + 1) * HEAD_DIM)

    @pl.when((pl.program_id(0) == 0) & (pl.program_id(1) == 0))
    def _():
        row = lax.broadcasted_iota(jnp.int32, (tb, tb), 0)
        col = lax.broadcasted_iota(jnp.int32, (tb, tb), 1)
        dist = jnp.abs(row - col).astype(F32)
        visible = (col // CHUNK) <= (row // CHUNK)
        ridx = lax.broadcasted_iota(jnp.int32, (tb, HEAD_DIM), 0).astype(F32)
        for h in range(HEADS):
            dmask_ref[h] = jnp.where(visible, jnp.exp(log_gamma[h] * dist) * key_scale, 0.0)
            qdec_ref[h] = jnp.exp(log_gamma[h] * (ridx + 1.0))
            kdec_ref[h] = jnp.exp(log_gamma[h] * (tb - 1.0 - ridx)) * key_scale

    @pl.when(pl.program_id(1) == 0)
    def _():
        rstate_ref[...] = jnp.zeros_like(rstate_ref)
        gstate_ref[...] = jnp.zeros_like(gstate_ref)
        xpad_a[0:CONV_PAD, :] = jnp.zeros((CONV_PAD, CONV_CH), F32)

    shift = mod_ref[0:1, :]
    scale = mod_ref[1:2, :]
    in_gain = nw_ref[...] * (1.0 + scale)

    def project_block(bufs, blk):
        ret_ref, xpad_ref, gz_ref, gab_ref = bufs
        h_in = (_rms(x_ref[blk, :]) * in_gain + shift).astype(BF16)

        def project(dest_ref, row0, col0, width):
            for c in range(0, width, PROJ_TILE):
                n = min(PROJ_TILE, width - c)
                w_tile = _unpack_rows(w_ref[:, col0 + c:col0 + c + n])
                dest_ref[row0:row0 + tb, c:c + n] = jnp.dot(h_in, w_tile, preferred_element_type=F32)

        project(gab_ref, 0, COL_GAB, GAB_W)
        project(xpad_ref, CONV_PAD, COL_CONV, CONV_CH)
        project(ret_ref, 0, COL_RET, COL_CONV - COL_RET)
        project(gz_ref, 0, COL_GZ, GROUP_W)

    def delta_front(bufs):
        _, xpad_ref, _, gab_ref = bufs
        xp = xpad_ref[...]
        conv = xp * cw_ref[0:1, :]
        for w in range(1, CONV_WIDTH):
            conv = xp * cw_ref[w:w + 1, :] + pltpu.roll(conv, 1, 0)
        qkv = _silu(conv[CONV_PAD:, :])

        gab = gab_ref[...]
        g = -jnp.exp(alog_ref[...]) * _softplus(gab + dtb_ref[...])
        beta = _sigmoid(gab)
        in_chunk = lax.broadcasted_iota(jnp.int32, (tb, GAB_W), 0) % CHUNK
        gc = g
        step = 1
        while step < CHUNK:
            gc = gc + jnp.where(in_chunk >= step, pltpu.roll(gc, step, 0), 0.0)
            step *= 2

        f = dict(gq=[], gk=[], gk_b=[], gv=[], gch=[], bh=[], e_gc=[], k_dec=[])
        for h in heads:
            qh = qkv[:, head_cols(0, h)]
            kh = qkv[:, head_cols(1, h)]
            f["gq"].append(qh * lax.rsqrt(jnp.sum(qh * qh, axis=-1, keepdims=True) + NORM_EPS) * (HEAD_DIM ** -0.5))
            gk = kh * lax.rsqrt(jnp.sum(kh * kh, axis=-1, keepdims=True) + NORM_EPS)
            f["gk"].append(gk)
            f["gk_b"].append(gk.astype(BF16))
            f["gv"].append(qkv[:, head_cols(2, h)])
            gch = jnp.broadcast_to(gc[:, h:h + 1], (tb, HEAD_DIM))
            f["gch"].append(gch)
            f["bh"].append(jnp.broadcast_to(beta[:, HEADS + h:HEADS + h + 1], (tb, HEAD_DIM)))
            f["e_gc"].append(jnp.exp(gch))
            g_last = jnp.concatenate(
                [jnp.broadcast_to(gch[(c + 1) * CHUNK - 1:(c + 1) * CHUNK, :], (CHUNK, HEAD_DIM))
                 for c in range(n_chunks)], axis=0)
            f["k_dec"].append(gk * jnp.exp(g_last - gch))
        return f

    def retention(bufs, blk):
        ret_ref = bufs[0]
        cos = cos_ref[blk, :]
        sin = sin_ref[blk, :]
        rq, rk, rvb = [], [], []
        for h in heads:
            q = ret_ref[:, head_cols(0, h)]
            k = ret_ref[:, head_cols(1, h)]
            rq.append(q * cos + pltpu.roll(q, HEAD_DIM // 2, 1) * sin)
            rk.append(k * cos + pltpu.roll(k, HEAD_DIM // 2, 1) * sin)
            rvb.append(ret_ref[:, head_cols(2, h)].astype(BF16))
        scores = [_dot_nt(rq[h], rk[h]) * dmask_ref[h] for h in heads]
        rstates = [rstate_ref[h] for h in heads]
        ro = [_dot(scores[h], rvb[h]) + _dot(rq[h] * qdec_ref[h], rstates[h]) for h in heads]
        for h in heads:
            rstate_ref[h] = rstates[h] * math.exp(log_gamma[h] * tb) + _dot_tn(rk[h] * kdec_ref[h], rvb[h])
        for h in heads:
            mu = jnp.mean(ro[h], axis=-1, keepdims=True)
            d = ro[h] - mu
            var = jnp.mean(d * d, axis=-1, keepdims=True)
            y = d * lax.rsqrt(var + NORM_EPS) * rnw_ref[:, head_cols(0, h)] * _silu(ret_ref[:, head_cols(3, h)])
            o_ref[blk, head_cols(0, h)] = y.astype(o_ref.dtype)

    project_block(*blocks[0])
    xpad_b[0:CONV_PAD, :] = xpad_a[tb:tb + CONV_PAD, :]
    project_block(*blocks[1])
    fronts = [delta_front(bufs) for bufs, _ in blocks]
    for bufs, blk in blocks:
        retention(bufs, blk)
    xpad_a[0:CONV_PAD, :] = xpad_b[tb:tb + CONV_PAD, :]

    pair = 2 * CHUNK
    units = [(i, h, p) for i in range(len(blocks)) for p in range(tb // pair) for h in heads]
    pcol = lax.broadcasted_iota(jnp.int32, (pair, pair), 1)
    frow = lax.broadcasted_iota(jnp.int32, (CHUNK, pair), 0)
    flane = lax.broadcasted_iota(jnp.int32, (CHUNK, pair), 1)
    first = flane < CHUNK
    causal = frow >= flane % CHUNK
    diag = frow == flane % CHUNK
    eye = jnp.where(diag, 1.0, 0.0)

    def fold(a):
        return jnp.where(first, a[:CHUNK], a[CHUNK:])

    def chunk_lanes(a, c):
        return jnp.where(first if c == 0 else ~first, a, jnp.zeros_like(a))

    def unfold(a):
        return jnp.concatenate([chunk_lanes(a, 0), chunk_lanes(a, 1)], axis=0)

    def block_diag_dot(a, b):
        return jnp.concatenate([_dot(chunk_lanes(a, 0), b), _dot(chunk_lanes(a, 1), b)], axis=0)

    def unit_rows(p):
        return slice(p * pair, (p + 1) * pair)

    def operand(name, i, h, p):
        return fronts[i][name][h][unit_rows(p)]

    kk = [fold(_dot_nt(operand("gk_b", *u), operand("gk_b", *u))) for u in units]
    qk = [fold(_dot_nt(operand("gq", *u), operand("gk_b", *u))) for u in units]
    decay, power, inv = [], [], []
    for n, u in enumerate(units):
        gm = operand("gch", *u)
        decay.append(jnp.exp(jnp.where(causal, fold(gm) - gm.T[:CHUNK], -jnp.inf)))
        power.append(jnp.where(diag, 0.0, -(fold(operand("bh", *u)) * kk[n] * decay[n])))
        inv.append(eye + power[n])
    level = 2
    while level < CHUNK:
        power = [_dot(pw, unfold(pw.astype(BF16))) for pw in power]
        inv = [iv + _dot(iv, unfold(pw.astype(BF16))) for iv, pw in zip(inv, power)]
        level *= 2
    uw = []
    for n, u in enumerate(units):
        b = operand("bh", *u)
        rhs = jnp.concatenate([b * operand("gv", *u), b * operand("gk", *u) * operand("e_gc", *u)], axis=1)
        uw.append(block_diag_dot(inv[n].astype(BF16), rhs.astype(BF16)))
    a_uw = [block_diag_dot((qk[n] * decay[n]).astype(BF16), uw[n].astype(BF16))
            for n in range(len(units))]
    q_eff, k_uw = [], []
    for n, u in enumerate(units):
        q_eff.append(operand("gq", *u) * operand("e_gc", *u) - a_uw[n][:, HEAD_DIM:])
        kd_t = operand("k_dec", *u).T
        k_uw.append([_dot(jnp.where(pcol // CHUNK == c, kd_t, 0.0), uw[n])
                     for c in range(pair // CHUNK)])
    gstates = [gstate_ref[h] for h in heads]
    for i, (bufs, blk) in enumerate(blocks):
        outs = [[] for _ in heads]
        for c in range(n_chunks):
            p, cl = divmod(c, pair // CHUNK)
            for h in heads:
                n = units.index((i, h, p))
                local = slice(cl * CHUNK, (cl + 1) * CHUNK)
                sb = gstates[h].astype(BF16)
                outs[h].append(_dot(q_eff[n][local], sb) + a_uw[n][local, :HEAD_DIM])
                gch = fronts[i]["gch"][h]
                chunk_decay = jnp.exp(gch[(c + 1) * CHUNK - 1:(c + 1) * CHUNK, :])
                gstates[h] = (gstates[h] * chunk_decay - _dot(k_uw[n][cl][:, HEAD_DIM:], sb)
                              + k_uw[n][cl][:, :HEAD_DIM])
        gz_ref = bufs[2]
        for h in heads:
            o = jnp.concatenate(outs[h], axis=0)
            y = _rms(o) * gnw_ref[...] * _silu(gz_ref[:, head_cols(0, h)])
            o_ref[blk, head_cols(1, h)] = y.astype(o_ref.dtype)
    for h in heads:
        gstate_ref[h] = gstates[h]


def _token_mix(x, mod, norm_w, w_proj, cos_t, sin_t, conv_w, a_log_pad, dt_pad, ret_norm_w, gdn_norm_w):
    batch, seq, d = x.shape
    tb = TIME_BLOCK
    step_rows = 2 * tb
    const = lambda shape: pl.BlockSpec(shape, lambda b, t: (0,) * len(shape))
    proj_bufs = [
        pltpu.VMEM((tb, 4 * GROUP_W), F32),
        pltpu.VMEM((tb + CONV_PAD, CONV_CH), F32),
        pltpu.VMEM((tb, GROUP_W), F32),
        pltpu.VMEM((tb, GAB_W), F32),
    ]
    return pl.pallas_call(
        _mix_kernel,
        grid=(batch, seq // step_rows),
        in_specs=[
            pl.BlockSpec((None, step_rows, d), lambda b, t: (b, t, 0)),
            pl.BlockSpec((None, N_MOD, d), lambda b, t: (b, 0, 0)),
            const((1, d)),
            pl.BlockSpec((d // 2, PROJ_W), lambda b, t: (0, 0), pipeline_mode=pl.Buffered(1)),
            pl.BlockSpec((step_rows, HEAD_DIM), lambda b, t: (t, 0)),
            pl.BlockSpec((step_rows, HEAD_DIM), lambda b, t: (t, 0)),
            const((CONV_WIDTH, CONV_CH)),
            const((1, GAB_W)),
            const((1, GAB_W)),
            const((1, GROUP_W)),
            const((1, HEAD_DIM)),
        ],
        out_specs=pl.BlockSpec((None, step_rows, 2 * GROUP_W), lambda b, t: (b, t, 0)),
        out_shape=jax.ShapeDtypeStruct((batch, seq, 2 * GROUP_W), BF16),
        scratch_shapes=proj_bufs + proj_bufs + [
            pltpu.VMEM((HEADS, HEAD_DIM, HEAD_DIM), F32),
            pltpu.VMEM((HEADS, HEAD_DIM, HEAD_DIM), F32),
            pltpu.VMEM((HEADS, tb, tb), F32),
            pltpu.VMEM((HEADS, tb, HEAD_DIM), F32),
            pltpu.VMEM((HEADS, tb, HEAD_DIM), F32),
        ],
        compiler_params=pltpu.CompilerParams(
            dimension_semantics=("arbitrary", "arbitrary"), vmem_limit_bytes=VMEM_LIMIT_BYTES),
        name="token_mix",
    )(x, mod, norm_w, w_proj, cos_t, sin_t, conv_w, a_log_pad, dt_pad, ret_norm_w, gdn_norm_w)


def _channel_kernel(x_ref, mixed_ref, mod_ref, wo_ref, nw_ref, w1_ref, w2_ref, fw_ref, o_ref):
    gate_a = mod_ref[2:3, :]
    shift = mod_ref[3:4, :]
    scale = mod_ref[4:5, :]
    gate_m = mod_ref[5:6, :]
    x1 = x_ref[...] + gate_a * jnp.dot(mixed_ref[...], _unpack_rows(wo_ref[...]), preferred_element_type=F32)
    h = (_rms(x1) * nw_ref[...] * (1.0 + scale) + shift).astype(BF16)
    d_ff = w1_ref.shape[1]
    acc = jnp.zeros(x1.shape, F32)
    for j in range(d_ff // FF_BLOCK):
        cols = slice(j * FF_BLOCK, (j + 1) * FF_BLOCK)
        a = jnp.maximum(jnp.dot(h, _unpack_rows(w1_ref[:, cols]), preferred_element_type=F32), 0.0)
        w2_rows = _unpack_rows(w2_ref[j * FF_BLOCK // 2:(j + 1) * FF_BLOCK // 2, :])
        acc = acc + jnp.dot((a * a).astype(BF16), w2_rows, preferred_element_type=F32)
    x2 = x1 + gate_m * acc
    o_ref[...] = _rms(x2) * fw_ref[...]


def _channel_mix(x, mixed, mod, w_out, norm_w, w_ff1, w_ff2, final_w):
    batch, seq, d = x.shape
    d_ff = w_ff1.shape[1]
    rb = ROW_BLOCK
    per_batch = seq // rb
    resident = lambda shape: pl.BlockSpec(shape, lambda i: (0, 0), pipeline_mode=pl.Buffered(1))
    out = pl.pallas_call(
        _channel_kernel,
        grid=(batch * per_batch,),
        in_specs=[
            pl.BlockSpec((rb, d), lambda i: (i, 0)),
            pl.BlockSpec((rb, mixed.shape[-1]), lambda i: (i, 0)),
            pl.BlockSpec((None, N_MOD, d), lambda i: (i // per_batch, 0, 0)),
            resident((mixed.shape[-1] // 2, d)),
            pl.BlockSpec((1, d), lambda i: (0, 0)),
            resident((d // 2, d_ff)),
            resident((d_ff // 2, d)),
            pl.BlockSpec((1, d), lambda i: (0, 0)),
        ],
        out_specs=pl.BlockSpec((rb, d), lambda i: (i, 0)),
        out_shape=jax.ShapeDtypeStruct((batch * seq, d), F32),
        compiler_params=pltpu.CompilerParams(
            dimension_semantics=("arbitrary",), vmem_limit_bytes=VMEM_LIMIT_BYTES),
        name="channel_mix",
    )(x.reshape(batch * seq, d), mixed.reshape(batch * seq, -1), mod, w_out, norm_w, w_ff1, w_ff2, final_w)
    return out.reshape(batch, seq, d)


def _pad_lanes(v, width):
    return jnp.pad(v, (0, width - v.shape[0])).reshape(1, width)


def kernel(x, c, ada_w, ada_b, norm_mix_w, w_in, conv_w, a_log, dt_bias, ret_norm_w, gdn_norm_w, w_out,
           norm_mlp_w, w_ff1, w_ff2, norm_final_w):
    batch, seq, d = x.shape
    assert ada_w.shape[0] == 1, "single-layer block: the final rmsnorm is fused into the channel-mix call"
    cos_t, sin_t = _rope_tables(seq)
    mod = _modulation(c, ada_w, ada_b).reshape(batch, N_MOD, d)
    w_proj = _pack_rows(jnp.swapaxes(w_in, 1, 2), PROJ_W, transposed=True)
    mixed = _token_mix(
        x, mod, norm_mix_w[0].reshape(1, d), w_proj, cos_t, sin_t, conv_w[0],
        _pad_lanes(a_log[0], GAB_W), _pad_lanes(dt_bias[0], GAB_W),
        ret_norm_w[0].reshape(1, GROUP_W), gdn_norm_w[0].reshape(1, HEAD_DIM))
    return _channel_mix(x, mixed, mod, _pack_rows(w_out), norm_mlp_w[0].reshape(1, d),
                        _pack_rows(w_ff1), _pack_rows(w_ff2), norm_final_w.reshape(1, d))
```

```python
import math

import jax
import jax.numpy as jnp
from jax import lax
from jax.experimental import pallas as pl
from jax.experimental.pallas import tpu as pltpu

CHUNK = 64
HEADS = 4
HEAD_DIM = 128
GROUP_W = HEADS * HEAD_DIM
CONV_WIDTH = 4
CONV_CH = 3 * GROUP_W
CONV_PAD = 8
ROPE_BASE = 10000.0
NORM_EPS = 1e-6
N_MOD = 6

TIME_BLOCK = 256
PROJ_TILE = 256
ROW_BLOCK = 1024
FF_BLOCK = 1024
ROPE_BLOCK = 512
LANES = 128
PACK_BLOCK_BYTES = 6 * 1024 * 1024
MIX_VMEM_LIMIT_BYTES = 48 * 1024 * 1024
CHANNEL_VMEM_LIMIT_BYTES = 56 * 1024 * 1024

COL_RET = 0
COL_CONV = 4 * GROUP_W
COL_GZ = COL_CONV + CONV_CH
COL_GAB = COL_GZ + GROUP_W
GAB_W = LANES
PROJ_W = COL_GAB + GAB_W

NT_DIMS = (((1,), (1,)), ((), ()))
TN_DIMS = (((0,), (0,)), ((), ()))

BF16 = jnp.bfloat16
F32 = jnp.float32


def _dot(a, b):
    return jnp.dot(a.astype(BF16), b.astype(BF16), preferred_element_type=F32)


def _dot_nt(a, b):
    return lax.dot_general(a.astype(BF16), b.astype(BF16), NT_DIMS, preferred_element_type=F32)


def _dot_tn(a, b):
    return lax.dot_general(a.astype(BF16), b.astype(BF16), TN_DIMS, preferred_element_type=F32)


def _pack_kernel(n_valid, col_axis, w_ref, o_ref):
    w = w_ref[...]
    tile = w.shape[col_axis]
    if n_valid % tile:
        col = pl.program_id(0) * tile + lax.broadcasted_iota(jnp.int32, w.shape, col_axis)
        w = jnp.where(col < n_valid, w, 0.0)
    if col_axis == 0:
        w = w.T
    o_ref[...] = pltpu.bitcast(w.astype(BF16), jnp.uint32)


def _pack_rows(w, n_out=None, transposed=False):
    k, n = (w.shape[2], w.shape[1]) if transposed else (w.shape[1], w.shape[2])
    n_out = n if n_out is None else n_out
    tile = max(t for t in range(LANES, n_out + 1, LANES)
               if n_out % t == 0 and (k * t * 4 <= PACK_BLOCK_BYTES or t == LANES))
    in_spec = (pl.BlockSpec((None, tile, k), lambda j: (0, j, 0)) if transposed
               else pl.BlockSpec((None, k, tile), lambda j: (0, 0, j)))
    return pl.pallas_call(
        lambda w_ref, o_ref: _pack_kernel(n, 0 if transposed else 1, w_ref, o_ref),
        grid=(n_out // tile,),
        in_specs=[in_spec],
        out_specs=pl.BlockSpec((k // 2, tile), lambda j: (0, j)),
        out_shape=jax.ShapeDtypeStruct((k // 2, n_out), jnp.uint32),
        name="pack_weight",
    )(w)


def _unpack_rows(w):
    return pltpu.bitcast(w, BF16)


def _sigmoid(x):
    return 1.0 / (1.0 + jnp.exp(-x))


def _silu(x):
    return x * _sigmoid(x)


def _softplus(x):
    return jnp.maximum(x, 0.0) + jnp.log(1.0 + jnp.exp(-jnp.abs(x)))


def _rms(x):
    return x * lax.rsqrt(jnp.mean(x * x, axis=-1, keepdims=True) + NORM_EPS)


def _mod_kernel(c_ref, w_ref, b_ref, o_ref):
    o_ref[...] = _dot(_silu(c_ref[...]), w_ref[...]) + b_ref[...]


def _modulation(c, ada_w, ada_b):
    batch, d = c.shape
    n = ada_w.shape[2]
    return pl.pallas_call(
        _mod_kernel,
        grid=(n // d,),
        in_specs=[
            pl.BlockSpec((batch, d), lambda j: (0, 0)),
            pl.BlockSpec((None, d, d), lambda j: (0, 0, j)),
            pl.BlockSpec((1, d), lambda j: (0, j)),
        ],
        out_specs=pl.BlockSpec((batch, d), lambda j: (0, j)),
        out_shape=jax.ShapeDtypeStruct((batch, n), F32),
        name="modulation",
    )(c, ada_w, ada_b)


def _rope_kernel(cos_ref, sin_ref):
    rows = cos_ref.shape[0]
    pos = (pl.program_id(0) * rows + lax.broadcasted_iota(jnp.int32, (rows, HEAD_DIM), 0)).astype(F32)
    lane = lax.broadcasted_iota(jnp.int32, (rows, HEAD_DIM), 1)
    half = HEAD_DIM // 2
    freq = jnp.where(lane < half, lane, lane - half).astype(F32)
    inv = jnp.exp(freq * (-2.0 * math.log(ROPE_BASE) / HEAD_DIM))
    ang = pos * inv
    cos_ref[...] = jnp.cos(ang)
    sin_ref[...] = jnp.where(lane < half, -jnp.sin(ang), jnp.sin(ang))


def _rope_tables(seq):
    spec = pl.BlockSpec((ROPE_BLOCK, HEAD_DIM), lambda i: (i, 0))
    shape = jax.ShapeDtypeStruct((seq, HEAD_DIM), F32)
    return pl.pallas_call(
        _rope_kernel, grid=(seq // ROPE_BLOCK,), in_specs=[], out_specs=[spec, spec],
        out_shape=[shape, shape], name="rope_tables")()


def _mix_kernel(x_ref, mod_ref, nw_ref, w_ref, cos_ref, sin_ref, cw_ref, alog_ref, dtb_ref,
                rnw_ref, gnw_ref, o_ref,
                ret_a, xpad_a, gz_a, gab_a, ret_b, xpad_b, gz_b, gab_b, rstate_ref, gstate_ref,
                dmask_ref, qdec_ref, kdec_ref):
    tb = ret_a.shape[0]
    n_chunks = tb // CHUNK
    log_gamma = [math.log(1.0 - 2.0 ** (-5.0 - h)) for h in range(HEADS)]
    key_scale = HEAD_DIM ** -0.5
    heads = range(HEADS)
    blocks = (((ret_a, xpad_a, gz_a, gab_a), slice(0, tb)), ((ret_b, xpad_b, gz_b, gab_b), slice(tb, 2 * tb)))

    def head_cols(group, h):
        return slice(group * GROUP_W + h * HEAD_DIM, group * GROUP_W + (h + 1) * HEAD_DIM)

    @pl.when((pl.program_id(0) == 0) & (pl.program_id(1) == 0))
    def _():
        row = lax.broadcasted_iota(jnp.int32, (tb, tb), 0)
        col = lax.broadcasted_iota(jnp.int32, (tb, tb), 1)
        dist = jnp.abs(row - col).astype(F32)
        visible = (col // CHUNK) <= (row // CHUNK)
        ridx = lax.broadcasted_iota(jnp.int32, (tb, HEAD_DIM), 0).astype(F32)
        for h in range(HEADS):
            dmask_ref[h] = jnp.where(visible, jnp.exp(log_gamma[h] * dist) * key_scale, 0.0)
            qdec_ref[h] = jnp.exp(log_gamma[h] * (ridx + 1.0))
            kdec_ref[h] = jnp.exp(log_gamma[h] * (tb - 1.0 - ridx)) * key_scale

    @pl.when(pl.program_id(1) == 0)
    def _():
        rstate_ref[...] = jnp.zeros_like(rstate_ref)
        gstate_ref[...] = jnp.zeros_like(gstate_ref)
        xpad_a[0:CONV_PAD, :] = jnp.zeros((CONV_PAD, CONV_CH), F32)

    shift = mod_ref[0:1, :]
    scale = mod_ref[1:2, :]
    in_gain = nw_ref[...] * (1.0 + scale)

    def project_block(bufs, blk):
        ret_ref, xpad_ref, gz_ref, gab_ref = bufs
        h_in = (_rms(x_ref[blk, :]) * in_gain + shift).astype(BF16)

        def project(dest_ref, row0, col0, width):
            for c in range(0, width, PROJ_TILE):
                n = min(PROJ_TILE, width - c)
                w_tile = _unpack_rows(w_ref[:, col0 + c:col0 + c + n])
                dest_ref[row0:row0 + tb, c:c + n] = jnp.dot(h_in, w_tile, preferred_element_type=F32)

        project(gab_ref, 0, COL_GAB, GAB_W)
        project(xpad_ref, CONV_PAD, COL_CONV, CONV_CH)
        project(ret_ref, 0, COL_RET, COL_CONV - COL_RET)
        project(gz_ref, 0, COL_GZ, GROUP_W)

    def delta_front(bufs):
        _, xpad_ref, _, gab_ref = bufs
        xp = xpad_ref[...]
        conv = xp * cw_ref[0:1, :]
        for w in range(1, CONV_WIDTH):
            conv = xp * cw_ref[w:w + 1, :] + pltpu.roll(conv, 1, 0)
        qkv = _silu(conv[CONV_PAD:, :])

        gab = gab_ref[...]
        g = -jnp.exp(alog_ref[...]) * _softplus(gab + dtb_ref[...])
        beta = _sigmoid(gab)
        in_chunk = lax.broadcasted_iota(jnp.int32, (tb, GAB_W), 0) % CHUNK
        gc = g
        step = 1
        while step < CHUNK:
            gc = gc + jnp.where(in_chunk >= step, pltpu.roll(gc, step, 0), 0.0)
            step *= 2

        f = dict(gq=[], gk=[], gk_b=[], gv=[], gch=[], bh=[], e_gc=[], k_dec=[])
        for h in heads:
            qh = qkv[:, head_cols(0, h)]
            kh = qkv[:, head_cols(1, h)]
            f["gq"].append(qh * lax.rsqrt(jnp.sum(qh * qh, axis=-1, keepdims=True) + NORM_EPS) * (HEAD_DIM ** -0.5))
            gk = kh * lax.rsqrt(jnp.sum(kh * kh, axis=-1, keepdims=True) + NORM_EPS)
            f["gk"].append(gk)
            f["gk_b"].append(gk.astype(BF16))
            f["gv"].append(qkv[:, head_cols(2, h)])
            gch = jnp.broadcast_to(gc[:, h:h + 1], (tb, HEAD_DIM))
            f["gch"].append(gch)
            f["bh"].append(jnp.broadcast_to(beta[:, HEADS + h:HEADS + h + 1], (tb, HEAD_DIM)))
            f["e_gc"].append(jnp.exp(gch))
            g_last = jnp.concatenate(
                [jnp.broadcast_to(gch[(c + 1) * CHUNK - 1:(c + 1) * CHUNK, :], (CHUNK, HEAD_DIM))
                 for c in range(n_chunks)], axis=0)
            f["k_dec"].append(gk * jnp.exp(g_last - gch))
        return f

    def retention(bufs, blk):
        ret_ref = bufs[0]
        cos = cos_ref[blk, :]
        sin = sin_ref[blk, :]
        rq, rk, rvb = [], [], []
        for h in heads:
            q = ret_ref[:, head_cols(0, h)]
            k = ret_ref[:, head_cols(1, h)]
            rq.append(q * cos + pltpu.roll(q, HEAD_DIM // 2, 1) * sin)
            rk.append(k * cos + pltpu.roll(k, HEAD_DIM // 2, 1) * sin)
            rvb.append(ret_ref[:, head_cols(2, h)].astype(BF16))
        scores = [_dot_nt(rq[h], rk[h]) * dmask_ref[h] for h in heads]
        rstates = [rstate_ref[h] for h in heads]
        ro = [_dot(scores[h], rvb[h]) + _dot(rq[h] * qdec_ref[h], rstates[h]) for h in heads]
        for h in heads:
            rstate_ref[h] = rstates[h] * math.exp(log_gamma[h] * tb) + _dot_tn(rk[h] * kdec_ref[h], rvb[h])
        for h in heads:
            mu = jnp.mean(ro[h], axis=-1, keepdims=True)
            d = ro[h] - mu
            var = jnp.mean(d * d, axis=-1, keepdims=True)
            y = d * lax.rsqrt(var + NORM_EPS) * rnw_ref[:, head_cols(0, h)] * _silu(ret_ref[:, head_cols(3, h)])
            o_ref[blk, head_cols(0, h)] = y.astype(o_ref.dtype)

    project_block(*blocks[0])
    xpad_b[0:CONV_PAD, :] = xpad_a[tb:tb + CONV_PAD, :]
    project_block(*blocks[1])
    fronts = [delta_front(bufs) for bufs, _ in blocks]
    for bufs, blk in blocks:
        retention(bufs, blk)
    xpad_a[0:CONV_PAD, :] = xpad_b[tb:tb + CONV_PAD, :]

    pair = 2 * CHUNK
    units = [(i, h, p) for i in range(len(blocks)) for p in range(tb // pair) for h in heads]
    pcol = lax.broadcasted_iota(jnp.int32, (pair, pair), 1)
    frow = lax.broadcasted_iota(jnp.int32, (CHUNK, pair), 0)
    flane = lax.broadcasted_iota(jnp.int32, (CHUNK, pair), 1)
    first = flane < CHUNK
    causal = frow >= flane % CHUNK
    diag = frow == flane % CHUNK
    eye = jnp.where(diag, 1.0, 0.0)

    def fold(a):
        return jnp.where(first, a[:CHUNK], a[CHUNK:])

    def chunk_lanes(a, c):
        return jnp.where(first if c == 0 else ~first, a, jnp.zeros_like(a))

    def unfold(a):
        return jnp.concatenate([chunk_lanes(a, 0), chunk_lanes(a, 1)], axis=0)

    def block_diag_dot(a, b):
        return jnp.concatenate([_dot(chunk_lanes(a, 0), b), _dot(chunk_lanes(a, 1), b)], axis=0)

    def unit_rows(p):
        return slice(p * pair, (p + 1) * pair)

    def operand(name, i, h, p):
        return fronts[i][name][h][unit_rows(p)]

    kk = [fold(_dot_nt(operand("gk_b", *u), operand("gk_b", *u))) for u in units]
    qk = [fold(_dot_nt(operand("gq", *u), operand("gk_b", *u))) for u in units]
    decay, power, inv = [], [], []
    for n, u in enumerate(units):
        gm = operand("gch", *u)
        decay.append(jnp.exp(jnp.where(causal, fold(gm) - gm.T[:CHUNK], -jnp.inf)))
        power.append(jnp.where(diag, 0.0, -(fold(operand("bh", *u)) * kk[n] * decay[n])))
        inv.append(eye + power[n])
    level = 2
    while level < CHUNK:
        power = [_dot(pw, unfold(pw.astype(BF16))) for pw in power]
        inv = [iv + _dot(iv, unfold(pw.astype(BF16))) for iv, pw in zip(inv, power)]
        level *= 2
    uw = []
    for n, u in enumerate(units):
        b = operand("bh", *u)
        rhs = jnp.concatenate([b * operand("gv", *u), b * operand("gk", *u) * operand("e_gc", *u)], axis=1)
        uw.append(block_diag_dot(inv[n].astype(BF16), rhs.astype(BF16)))
    a_uw = [block_diag_dot((qk[n] * decay[n]).astype(BF16), uw[n].astype(BF16))
            for n in range(len(units))]
    q_eff, k_uw = [], []
    for n, u in enumerate(units):
        q_eff.append(operand("gq", *u) * operand("e_gc", *u) - a_uw[n][:, HEAD_DIM:])
        kd_t = operand("k_dec", *u).T
        k_uw.append([_dot(jnp.where(pcol // CHUNK == c, kd_t, 0.0), uw[n])
                     for c in range(pair // CHUNK)])
    gstates = [gstate_ref[h] for h in heads]
    for i, (bufs, blk) in enumerate(blocks):
        outs = [[] for _ in heads]
        for c in range(n_chunks):
            p, cl = divmod(c, pair // CHUNK)
            for h in heads:
                n = units.index((i, h, p))
                local = slice(cl * CHUNK, (cl + 1) * CHUNK)
                sb = gstates[h].astype(BF16)
                outs[h].append(_dot(q_eff[n][local], sb) + a_uw[n][local, :HEAD_DIM])
                gch = fronts[i]["gch"][h]
                chunk_decay = jnp.exp(gch[(c + 1) * CHUNK - 1:(c + 1) * CHUNK, :])
                gstates[h] = (gstates[h] * chunk_decay - _dot(k_uw[n][cl][:, HEAD_DIM:], sb)
                              + k_uw[n][cl][:, :HEAD_DIM])
        gz_ref = bufs[2]
        for h in heads:
            o = jnp.concatenate(outs[h], axis=0)
            y = _rms(o) * gnw_ref[...] * _silu(gz_ref[:, head_cols(0, h)])
            o_ref[blk, head_cols(1, h)] = y.astype(o_ref.dtype)
    for h in heads:
        gstate_ref[h] = gstates[h]


def _token_mix(x, mod, norm_w, w_proj, cos_t, sin_t, conv_w, a_log_pad, dt_pad, ret_norm_w, gdn_norm_w):
    batch, seq, d = x.shape
    tb = TIME_BLOCK
    step_rows = 2 * tb
    const = lambda shape: pl.BlockSpec(shape, lambda b, t: (0,) * len(shape))
    proj_bufs = [
        pltpu.VMEM((tb, 4 * GROUP_W), F32),
        pltpu.VMEM((tb + CONV_PAD, CONV_CH), F32),
        pltpu.VMEM((tb, GROUP_W), F32),
        pltpu.VMEM((tb, GAB_W), F32),
    ]
    return pl.pallas_call(
        _mix_kernel,
        grid=(batch, seq // step_rows),
        in_specs=[
            pl.BlockSpec((None, step_rows, d), lambda b, t: (b, t, 0)),
            pl.BlockSpec((None, N_MOD, d), lambda b, t: (b, 0, 0)),
            const((1, d)),
            pl.BlockSpec((d // 2, PROJ_W), lambda b, t: (0, 0), pipeline_mode=pl.Buffered(1)),
            pl.BlockSpec((step_rows, HEAD_DIM), lambda b, t: (t, 0)),
            pl.BlockSpec((step_rows, HEAD_DIM), lambda b, t: (t, 0)),
            const((CONV_WIDTH, CONV_CH)),
            const((1, GAB_W)),
            const((1, GAB_W)),
            const((1, GROUP_W)),
            const((1, HEAD_DIM)),
        ],
        out_specs=pl.BlockSpec((None, step_rows, 2 * GROUP_W), lambda b, t: (b, t, 0)),
        out_shape=jax.ShapeDtypeStruct((batch, seq, 2 * GROUP_W), BF16),
        scratch_shapes=proj_bufs + proj_bufs + [
            pltpu.VMEM((HEADS, HEAD_DIM, HEAD_DIM), F32),
            pltpu.VMEM((HEADS, HEAD_DIM, HEAD_DIM), F32),
            pltpu.VMEM((HEADS, tb, tb), F32),
            pltpu.VMEM((HEADS, tb, HEAD_DIM), F32),
            pltpu.VMEM((HEADS, tb, HEAD_DIM), F32),
        ],
        compiler_params=pltpu.CompilerParams(
            dimension_semantics=("arbitrary", "arbitrary"), vmem_limit_bytes=MIX_VMEM_LIMIT_BYTES),
        name="token_mix",
    )(x, mod, norm_w, w_proj, cos_t, sin_t, conv_w, a_log_pad, dt_pad, ret_norm_w, gdn_norm_w)


def _channel_kernel(x_ref, mixed_ref, mod_ref, wo_ref, nw_ref, w1_ref, w2_ref, fw_ref, o_ref):
    gate_a = mod_ref[2:3, :]
    shift = mod_ref[3:4, :]
    scale = mod_ref[4:5, :]
    gate_m = mod_ref[5:6, :]
    x1 = x_ref[...] + gate_a * jnp.dot(mixed_ref[...], _unpack_rows(wo_ref[...]), preferred_element_type=F32)
    h = (_rms(x1) * nw_ref[...] * (1.0 + scale) + shift).astype(BF16)
    d_ff = w1_ref.shape[1]
    acc = jnp.zeros(x1.shape, F32)
    for j in range(d_ff // FF_BLOCK):
        cols = slice(j * FF_BLOCK, (j + 1) * FF_BLOCK)
        a = jnp.maximum(jnp.dot(h, _unpack_rows(w1_ref[:, cols]), preferred_element_type=F32), 0.0)
        w2_rows = _unpack_rows(w2_ref[j * FF_BLOCK // 2:(j + 1) * FF_BLOCK // 2, :])
        acc = acc + jnp.dot((a * a).astype(BF16), w2_rows, preferred_element_type=F32)
    x2 = x1 + gate_m * acc
    o_ref[...] = _rms(x2) * fw_ref[...]


def _channel_mix(x, mixed, mod, w_out, norm_w, w_ff1, w_ff2, final_w):
    batch, seq, d = x.shape
    d_ff = w_ff1.shape[1]
    rb = ROW_BLOCK
    per_batch = seq // rb
    resident = lambda shape: pl.BlockSpec(shape, lambda i: (0, 0), pipeline_mode=pl.Buffered(1))
    out = pl.pallas_call(
        _channel_kernel,
        grid=(batch * per_batch,),
        in_specs=[
            pl.BlockSpec((rb, d), lambda i: (i, 0)),
            pl.BlockSpec((rb, mixed.shape[-1]), lambda i: (i, 0)),
            pl.BlockSpec((None, N_MOD, d), lambda i: (i // per_batch, 0, 0)),
            resident((mixed.shape[-1] // 2, d)),
            pl.BlockSpec((1, d), lambda i: (0, 0)),
            resident((d // 2, d_ff)),
            resident((d_ff // 2, d)),
            pl.BlockSpec((1, d), lambda i: (0, 0)),
        ],
        out_specs=pl.BlockSpec((rb, d), lambda i: (i, 0)),
        out_shape=jax.ShapeDtypeStruct((batch * seq, d), F32),
        compiler_params=pltpu.CompilerParams(
            dimension_semantics=("arbitrary",), vmem_limit_bytes=CHANNEL_VMEM_LIMIT_BYTES),
        name="channel_mix",
    )(x.reshape(batch * seq, d), mixed.reshape(batch * seq, -1), mod, w_out, norm_w, w_ff1, w_ff2, final_w)
    return out.reshape(batch, seq, d)


def _pad_lanes(v, width):
    return jnp.pad(v, (0, width - v.shape[0])).reshape(1, width)


def kernel(x, c, ada_w, ada_b, norm_mix_w, w_in, conv_w, a_log, dt_bias, ret_norm_w, gdn_norm_w, w_out,
           norm_mlp_w, w_ff1, w_ff2, norm_final_w):
    batch, seq, d = x.shape
    assert ada_w.shape[0] == 1, "single-layer block: the final rmsnorm is fused into the channel-mix call"
    cos_t, sin_t = _rope_tables(seq)
    mod = _modulation(c, ada_w, ada_b).reshape(batch, N_MOD, d)
    w_proj = _pack_rows(jnp.swapaxes(w_in, 1, 2), PROJ_W, transposed=True)
    mixed = _token_mix(
        x, mod, norm_mix_w[0].reshape(1, d), w_proj, cos_t, sin_t, conv_w[0],
        _pad_lanes(a_log[0], GAB_W), _pad_lanes(dt_bias[0], GAB_W),
        ret_norm_w[0].reshape(1, GROUP_W), gdn_norm_w[0].reshape(1, HEAD_DIM))
    return _channel_mix(x, mixed, mod, _pack_rows(w_out), norm_mlp_w[0].reshape(1, d),
                        _pack_rows(w_ff1), _pack_rows(w_ff2), norm_final_w.reshape(1, d))
```

```python
import math

import jax
import jax.numpy as jnp
from jax import lax
from jax.experimental import pallas as pl
from jax.experimental.pallas import tpu as pltpu

CHUNK = 64
HEADS = 4
HEAD_DIM = 128
GROUP_W = HEADS * HEAD_DIM
CONV_WIDTH = 4
CONV_CH = 3 * GROUP_W
CONV_PAD = 8
ROPE_BASE = 10000.0
NORM_EPS = 1e-6
N_MOD = 6

TIME_BLOCK = 256
PROJ_TILE = 256
ROW_BLOCK = 1024
FF_BLOCK = 1024
ROPE_BLOCK = 512
LANES = 128
PACK_BLOCK_BYTES = 6 * 1024 * 1024
MIX_VMEM_LIMIT_BYTES = 48 * 1024 * 1024
CHANNEL_VMEM_LIMIT_BYTES = 56 * 1024 * 1024

COL_RET = 0
COL_CONV = 4 * GROUP_W
COL_GZ = COL_CONV + CONV_CH
COL_GAB = COL_GZ + GROUP_W
GAB_W = LANES
PROJ_W = COL_GAB + GAB_W

NT_DIMS = (((1,), (1,)), ((), ()))
TN_DIMS = (((0,), (0,)), ((), ()))

BF16 = jnp.bfloat16
F32 = jnp.float32


def _dot(a, b):
    return jnp.dot(a.astype(BF16), b.astype(BF16), preferred_element_type=F32)


def _dot_nt(a, b):
    return lax.dot_general(a.astype(BF16), b.astype(BF16), NT_DIMS, preferred_element_type=F32)


def _dot_tn(a, b):
    return lax.dot_general(a.astype(BF16), b.astype(BF16), TN_DIMS, preferred_element_type=F32)


def _pack_kernel(n_valid, col_axis, w_ref, o_ref):
    w = w_ref[...]
    tile = w.shape[col_axis]
    if n_valid % tile:
        col = pl.program_id(0) * tile + lax.broadcasted_iota(jnp.int32, w.shape, col_axis)
        w = jnp.where(col < n_valid, w, 0.0)
    if col_axis == 0:
        w = w.T
    o_ref[...] = pltpu.bitcast(w.astype(BF16), jnp.uint32)


def _pack_rows(w, n_out=None, transposed=False):
    k, n = (w.shape[2], w.shape[1]) if transposed else (w.shape[1], w.shape[2])
    n_out = n if n_out is None else n_out
    tile = max(t for t in range(LANES, n_out + 1, LANES)
               if n_out % t == 0 and (k * t * 4 <= PACK_BLOCK_BYTES or t == LANES))
    in_spec = (pl.BlockSpec((None, tile, k), lambda j: (0, j, 0)) if transposed
               else pl.BlockSpec((None, k, tile), lambda j: (0, 0, j)))
    return pl.pallas_call(
        lambda w_ref, o_ref: _pack_kernel(n, 0 if transposed else 1, w_ref, o_ref),
        grid=(n_out // tile,),
        in_specs=[in_spec],
        out_specs=pl.BlockSpec((k // 2, tile), lambda j: (0, j)),
        out_shape=jax.ShapeDtypeStruct((k // 2, n_out), jnp.uint32),
        name="pack_weight",
    )(w)


def _unpack_rows(w):
    return pltpu.bitcast(w, BF16)


def _sigmoid(x):
    return 1.0 / (1.0 + jnp.exp(-x))


def _silu(x):
    return x * _sigmoid(x)


def _softplus(x):
    return jnp.maximum(x, 0.0) + jnp.log(1.0 + jnp.exp(-jnp.abs(x)))


def _rms(x):
    return x * lax.rsqrt(jnp.mean(x * x, axis=-1, keepdims=True) + NORM_EPS)


def _mod_kernel(c_ref, w_ref, b_ref, o_ref):
    o_ref[...] = _dot(_silu(c_ref[...]), w_ref[...]) + b_ref[...]


def _modulation(c, ada_w, ada_b):
    batch, d = c.shape
    n = ada_w.shape[2]
    return pl.pallas_call(
        _mod_kernel,
        grid=(n // d,),
        in_specs=[
            pl.BlockSpec((batch, d), lambda j: (0, 0)),
            pl.BlockSpec((None, d, d), lambda j: (0, 0, j)),
            pl.BlockSpec((1, d), lambda j: (0, j)),
        ],
        out_specs=pl.BlockSpec((batch, d), lambda j: (0, j)),
        out_shape=jax.ShapeDtypeStruct((batch, n), F32),
        name="modulation",
    )(c, ada_w, ada_b)


def _rope_kernel(cos_ref, sin_ref):
    rows = cos_ref.shape[0]
    pos = (pl.program_id(0) * rows + lax.broadcasted_iota(jnp.int32, (rows, HEAD_DIM), 0)).astype(F32)
    lane = lax.broadcasted_iota(jnp.int32, (rows, HEAD_DIM), 1)
    half = HEAD_DIM // 2
    freq = jnp.where(lane < half, lane, lane - half).astype(F32)
    inv = jnp.exp(freq * (-2.0 * math.log(ROPE_BASE) / HEAD_DIM))
    ang = pos * inv
    cos_ref[...] = jnp.cos(ang)
    sin_ref[...] = jnp.where(lane < half, -jnp.sin(ang), jnp.sin(ang))


def _rope_tables(seq):
    spec = pl.BlockSpec((ROPE_BLOCK, HEAD_DIM), lambda i: (i, 0))
    shape = jax.ShapeDtypeStruct((seq, HEAD_DIM), F32)
    return pl.pallas_call(
        _rope_kernel, grid=(seq // ROPE_BLOCK,), in_specs=[], out_specs=[spec, spec],
        out_shape=[shape, shape], name="rope_tables")()


def _mix_kernel(x_ref, mod_ref, nw_ref, w_ref, cos_ref, sin_ref, cw_ref, alog_ref, dtb_ref,
                rnw_ref, gnw_ref, o_ref,
                ret_a, xpad_a, gz_a, gab_a, ret_b, xpad_b, gz_b, gab_b, rstate_ref, gstate_ref,
                dmask_ref, qdec_ref, kdec_ref):
    tb = ret_a.shape[0]
    n_chunks = tb // CHUNK
    log_gamma = [math.log(1.0 - 2.0 ** (-5.0 - h)) for h in range(HEADS)]
    key_scale = HEAD_DIM ** -0.5
    heads = range(HEADS)
    blocks = (((ret_a, xpad_a, gz_a, gab_a), slice(0, tb)), ((ret_b, xpad_b, gz_b, gab_b), slice(tb, 2 * tb)))

    def head_cols(group, h):
        return slice(group * GROUP_W + h * HEAD_DIM, group * GROUP_W + (h + 1) * HEAD_DIM)

    @pl.when((pl.program_id(0) == 0) & (pl.program_id(1) == 0))
    def _():
        row = lax.broadcasted_iota(jnp.int32, (tb, tb), 0)
        col = lax.broadcasted_iota(jnp.int32, (tb, tb), 1)
        dist = jnp.abs(row - col).astype(F32)
        visible = (col // CHUNK) <= (row // CHUNK)
        ridx = lax.broadcasted_iota(jnp.int32, (tb, HEAD_DIM), 0).astype(F32)
        for h in range(HEADS):
            dmask_ref[h] = jnp.where(visible, jnp.exp(log_gamma[h] * dist) * key_scale, 0.0)
            qdec_ref[h] = jnp.exp(log_gamma[h] * (ridx + 1.0))
            kdec_ref[h] = jnp.exp(log_gamma[h] * (tb - 1.0 - ridx)) * key_scale

    @pl.when(pl.program_id(1) == 0)
    def _():
        rstate_ref[...] = jnp.zeros_like(rstate_ref)
        gstate_ref[...] = jnp.zeros_like(gstate_ref)
        xpad_a[0:CONV_PAD, :] = jnp.zeros((CONV_PAD, CONV_CH), F32)

    shift = mod_ref[0:1, :]
    scale = mod_ref[1:2, :]
    in_gain = nw_ref[...] * (1.0 + scale)

    def project_block(bufs, blk):
        ret_ref, xpad_ref, gz_ref, gab_ref = bufs
        h_in = (_rms(x_ref[blk, :]) * in_gain + shift).astype(BF16)

        def project(dest_ref, row0, col0, width):
            for c in range(0, width, PROJ_TILE):
                n = min(PROJ_TILE, width - c)
                w_tile = _unpack_rows(w_ref[:, col0 + c:col0 + c + n])
                dest_ref[row0:row0 + tb, c:c + n] = jnp.dot(h_in, w_tile, preferred_element_type=F32)

        project(gab_ref, 0, COL_GAB, GAB_W)
        project(xpad_ref, CONV_PAD, COL_CONV, CONV_CH)
        project(ret_ref, 0, COL_RET, COL_CONV - COL_RET)
        project(gz_ref, 0, COL_GZ, GROUP_W)

    def delta_front(bufs):
        _, xpad_ref, _, gab_ref = bufs
        xp = xpad_ref[...]
        conv = xp * cw_ref[0:1, :]
        for w in range(1, CONV_WIDTH):
            conv = xp * cw_ref[w:w + 1, :] + pltpu.roll(conv, 1, 0)
        qkv = _silu(conv[CONV_PAD:, :])

        gab = gab_ref[...]
        g = -jnp.exp(alog_ref[...]) * _softplus(gab + dtb_ref[...])
        beta = _sigmoid(gab)
        in_chunk = lax.broadcasted_iota(jnp.int32, (tb, GAB_W), 0) % CHUNK
        gc = g
        step = 1
        while step < CHUNK:
            gc = gc + jnp.where(in_chunk >= step, pltpu.roll(gc, step, 0), 0.0)
            step *= 2

        f = dict(gq=[], gk=[], gk_b=[], gv=[], gch=[], bh=[], e_gc=[], k_dec=[])
        for h in heads:
            qh = qkv[:, head_cols(0, h)]
            kh = qkv[:, head_cols(1, h)]
            f["gq"].append(qh * lax.rsqrt(jnp.sum(qh * qh, axis=-1, keepdims=True) + NORM_EPS) * (HEAD_DIM ** -0.5))
            gk = kh * lax.rsqrt(jnp.sum(kh * kh, axis=-1, keepdims=True) + NORM_EPS)
            f["gk"].append(gk)
            f["gk_b"].append(gk.astype(BF16))
            f["gv"].append(qkv[:, head_cols(2, h)])
            gch = jnp.broadcast_to(gc[:, h:h + 1], (tb, HEAD_DIM))
            f["gch"].append(gch)
            f["bh"].append(jnp.broadcast_to(beta[:, HEADS + h:HEADS + h + 1], (tb, HEAD_DIM)))
            f["e_gc"].append(jnp.exp(gch))
            g_last = jnp.concatenate(
                [jnp.broadcast_to(gch[(c + 1) * CHUNK - 1:(c + 1) * CHUNK, :], (CHUNK, HEAD_DIM))
                 for c in range(n_chunks)], axis=0)
            f["k_dec"].append(gk * jnp.exp(g_last - gch))
        return f

    def retention(bufs, blk):
        ret_ref = bufs[0]
        cos = cos_ref[blk, :]
        sin = sin_ref[blk, :]
        rq, rk, rvb = [], [], []
        for h in heads:
            q = ret_ref[:, head_cols(0, h)]
            k = ret_ref[:, head_cols(1, h)]
            rq.append(q * cos + pltpu.roll(q, HEAD_DIM // 2, 1) * sin)
            rk.append(k * cos + pltpu.roll(k, HEAD_DIM // 2, 1) * sin)
            rvb.append(ret_ref[:, head_cols(2, h)].astype(BF16))
        scores = [_dot_nt(rq[h], rk[h]) * dmask_ref[h] for h in heads]
        rstates = [rstate_ref[h] for h in heads]
        ro = [_dot(scores[h], rvb[h]) + _dot(rq[h] * qdec_ref[h], rstates[h]) for h in heads]
        for h in heads:
            rstate_ref[h] = rstates[h] * math.exp(log_gamma[h] * tb) + _dot_tn(rk[h] * kdec_ref[h], rvb[h])
        for h in heads:
            mu = jnp.mean(ro[h], axis=-1, keepdims=True)
            d = ro[h] - mu
            var = jnp.mean(d * d, axis=-1, keepdims=True)
            y = d * lax.rsqrt(var + NORM_EPS) * rnw_ref[:, head_cols(0, h)] * _silu(ret_ref[:, head_cols(3, h)])
            o_ref[blk, head_cols(0, h)] = y.astype(o_ref.dtype)

    project_block(*blocks[0])
    xpad_b[0:CONV_PAD, :] = xpad_a[tb:tb + CONV_PAD, :]
    project_block(*blocks[1])
    fronts = [delta_front(bufs) for bufs, _ in blocks]
    for bufs, blk in blocks:
        retention(bufs, blk)
    xpad_a[0:CONV_PAD, :] = xpad_b[tb:tb + CONV_PAD, :]

    pair = 2 * CHUNK
    units = [(i, h, p) for i in range(len(blocks)) for p in range(tb // pair) for h in heads]
    pcol = lax.broadcasted_iota(jnp.int32, (pair, pair), 1)
    frow = lax.broadcasted_iota(jnp.int32, (CHUNK, pair), 0)
    flane = lax.broadcasted_iota(jnp.int32, (CHUNK, pair), 1)
    first = flane < CHUNK
    causal = frow >= flane % CHUNK
    diag = frow == flane % CHUNK
    eye = jnp.where(diag, 1.0, 0.0)

    def fold(a):
        return jnp.where(first, a[:CHUNK], a[CHUNK:])

    def chunk_lanes(a, c):
        return jnp.where(first if c == 0 else ~first, a, jnp.zeros_like(a))

    def unfold(a):
        return jnp.concatenate([chunk_lanes(a, 0), chunk_lanes(a, 1)], axis=0)

    def block_diag_dot(a, b):
        return jnp.concatenate([_dot(chunk_lanes(a, 0), b), _dot(chunk_lanes(a, 1), b)], axis=0)

    def unit_rows(p):
        return slice(p * pair, (p + 1) * pair)

    def operand(name, i, h, p):
        return fronts[i][name][h][unit_rows(p)]

    kk = [fold(_dot_nt(operand("gk_b", *u), operand("gk_b", *u))) for u in units]
    qk = [fold(_dot_nt(operand("gq", *u), operand("gk_b", *u))) for u in units]
    decay, power, inv = [], [], []
    for n, u in enumerate(units):
        gm = operand("gch", *u)
        decay.append(jnp.exp(jnp.where(causal, fold(gm) - gm.T[:CHUNK], -jnp.inf)))
        power.append(jnp.where(diag, 0.0, -(fold(operand("bh", *u)) * kk[n] * decay[n])))
        inv.append(eye + power[n])
    level = 2
    while level < CHUNK:
        power = [_dot(pw, unfold(pw.astype(BF16))) for pw in power]
        inv = [iv + _dot(iv, unfold(pw.astype(BF16))) for iv, pw in zip(inv, power)]
        level *= 2
    uw = []
    for n, u in enumerate(units):
        b = operand("bh", *u)
        rhs = jnp.concatenate([b * operand("gv", *u), b * operand("gk", *u) * operand("e_gc", *u)], axis=1)
        uw.append(block_diag_dot(inv[n].astype(BF16), rhs.astype(BF16)))
    a_uw = [block_diag_dot((qk[n] * decay[n]).astype(BF16), uw[n].astype(BF16))
            for n in range(len(units))]
    q_eff, k_uw = [], []
    for n, u in enumerate(units):
        q_eff.append(operand("gq", *u) * operand("e_gc", *u) - a_uw[n][:, HEAD_DIM:])
        kd_t = operand("k_dec", *u).T
        k_uw.append([_dot(jnp.where(pcol // CHUNK == c, kd_t, 0.0), uw[n])
                     for c in range(pair // CHUNK)])
    gstates = [gstate_ref[h] for h in heads]
    for i, (bufs, blk) in enumerate(blocks):
        outs = [[] for _ in heads]
        for c in range(n_chunks):
            p, cl = divmod(c, pair // CHUNK)
            for h in heads:
                n = units.index((i, h, p))
                local = slice(cl * CHUNK, (cl + 1) * CHUNK)
                sb = gstates[h].astype(BF16)
                outs[h].append(_dot(q_eff[n][local], sb) + a_uw[n][local, :HEAD_DIM])
                gch = fronts[i]["gch"][h]
                chunk_decay = jnp.exp(gch[(c + 1) * CHUNK - 1:(c + 1) * CHUNK, :])
                gstates[h] = (gstates[h] * chunk_decay - _dot(k_uw[n][cl][:, HEAD_DIM:], sb)
                              + k_uw[n][cl][:, :HEAD_DIM])
        gz_ref = bufs[2]
        for h in heads:
            o = jnp.concatenate(outs[h], axis=0)
            y = _rms(o) * gnw_ref[...] * _silu(gz_ref[:, head_cols(0, h)])
            o_ref[blk, head_cols(1, h)] = y.astype(o_ref.dtype)
    for h in heads:
        gstate_ref[h] = gstates[h]


def _token_mix(x, mod, norm_w, w_proj, cos_t, sin_t, conv_w, a_log_pad, dt_pad, ret_norm_w, gdn_norm_w):
    batch, seq, d = x.shape
    tb = TIME_BLOCK
    step_rows = 2 * tb
    assert seq % step_rows == 0 and d % 2 == 0 and w_proj.shape == (d // 2, PROJ_W)
    const = lambda shape: pl.BlockSpec(shape, lambda b, t: (0,) * len(shape))
    proj_bufs = [
        pltpu.VMEM((tb, 4 * GROUP_W), F32),
        pltpu.VMEM((tb + CONV_PAD, CONV_CH), F32),
        pltpu.VMEM((tb, GROUP_W), F32),
        pltpu.VMEM((tb, GAB_W), F32),
    ]
    return pl.pallas_call(
        _mix_kernel,
        grid=(batch, seq // step_rows),
        in_specs=[
            pl.BlockSpec((None, step_rows, d), lambda b, t: (b, t, 0)),
            pl.BlockSpec((None, N_MOD, d), lambda b, t: (b, 0, 0)),
            const((1, d)),
            pl.BlockSpec((d // 2, PROJ_W), lambda b, t: (0, 0), pipeline_mode=pl.Buffered(1)),
            pl.BlockSpec((step_rows, HEAD_DIM), lambda b, t: (t, 0)),
            pl.BlockSpec((step_rows, HEAD_DIM), lambda b, t: (t, 0)),
            const((CONV_WIDTH, CONV_CH)),
            const((1, GAB_W)),
            const((1, GAB_W)),
            const((1, GROUP_W)),
            const((1, HEAD_DIM)),
        ],
        out_specs=pl.BlockSpec((None, step_rows, 2 * GROUP_W), lambda b, t: (b, t, 0)),
        out_shape=jax.ShapeDtypeStruct((batch, seq, 2 * GROUP_W), BF16),
        scratch_shapes=proj_bufs + proj_bufs + [
            pltpu.VMEM((HEADS, HEAD_DIM, HEAD_DIM), F32),
            pltpu.VMEM((HEADS, HEAD_DIM, HEAD_DIM), F32),
            pltpu.VMEM((HEADS, tb, tb), F32),
            pltpu.VMEM((HEADS, tb, HEAD_DIM), F32),
            pltpu.VMEM((HEADS, tb, HEAD_DIM), F32),
        ],
        compiler_params=pltpu.CompilerParams(
            dimension_semantics=("arbitrary", "arbitrary"), vmem_limit_bytes=MIX_VMEM_LIMIT_BYTES),
        name="token_mix",
    )(x, mod, norm_w, w_proj, cos_t, sin_t, conv_w, a_log_pad, dt_pad, ret_norm_w, gdn_norm_w)


def _channel_kernel(x_ref, mixed_ref, mod_ref, wo_ref, nw_ref, w1_ref, w2_ref, fw_ref, o_ref):
    gate_a = mod_ref[2:3, :]
    shift = mod_ref[3:4, :]
    scale = mod_ref[4:5, :]
    gate_m = mod_ref[5:6, :]
    x1 = x_ref[...] + gate_a * jnp.dot(mixed_ref[...], _unpack_rows(wo_ref[...]), preferred_element_type=F32)
    h = (_rms(x1) * nw_ref[...] * (1.0 + scale) + shift).astype(BF16)
    d_ff = w1_ref.shape[1]
    acc = jnp.zeros(x1.shape, F32)
    for j in range(d_ff // FF_BLOCK):
        cols = slice(j * FF_BLOCK, (j + 1) * FF_BLOCK)
        a = jnp.maximum(jnp.dot(h, _unpack_rows(w1_ref[:, cols]), preferred_element_type=F32), 0.0)
        w2_rows = _unpack_rows(w2_ref[j * FF_BLOCK // 2:(j + 1) * FF_BLOCK // 2, :])
        acc = acc + jnp.dot((a * a).astype(BF16), w2_rows, preferred_element_type=F32)
    x2 = x1 + gate_m * acc
    o_ref[...] = _rms(x2) * fw_ref[...]


def _channel_mix(x, mixed, mod, w_out, norm_w, w_ff1, w_ff2, final_w):
    batch, seq, d = x.shape
    d_ff = w_ff1.shape[1]
    rb = ROW_BLOCK
    assert seq % rb == 0 and d_ff % FF_BLOCK == 0
    per_batch = seq // rb
    resident = lambda shape: pl.BlockSpec(shape, lambda i: (0, 0), pipeline_mode=pl.Buffered(1))
    out = pl.pallas_call(
        _channel_kernel,
        grid=(batch * per_batch,),
        in_specs=[
            pl.BlockSpec((rb, d), lambda i: (i, 0)),
            pl.BlockSpec((rb, mixed.shape[-1]), lambda i: (i, 0)),
            pl.BlockSpec((None, N_MOD, d), lambda i: (i // per_batch, 0, 0)),
            resident((mixed.shape[-1] // 2, d)),
            pl.BlockSpec((1, d), lambda i: (0, 0)),
            resident((d // 2, d_ff)),
            resident((d_ff // 2, d)),
            pl.BlockSpec((1, d), lambda i: (0, 0)),
        ],
        out_specs=pl.BlockSpec((rb, d), lambda i: (i, 0)),
        out_shape=jax.ShapeDtypeStruct((batch * seq, d), F32),
        compiler_params=pltpu.CompilerParams(
            dimension_semantics=("arbitrary",), vmem_limit_bytes=CHANNEL_VMEM_LIMIT_BYTES),
        name="channel_mix",
    )(x.reshape(batch * seq, d), mixed.reshape(batch * seq, -1), mod, w_out, norm_w, w_ff1, w_ff2, final_w)
    return out.reshape(batch, seq, d)


def _pad_lanes(v, width):
    return jnp.pad(v, (0, width - v.shape[0])).reshape(1, width)


def kernel(x, c, ada_w, ada_b, norm_mix_w, w_in, conv_w, a_log, dt_bias, ret_norm_w, gdn_norm_w, w_out,
           norm_mlp_w, w_ff1, w_ff2, norm_final_w):
    batch, seq, d = x.shape
    assert ada_w.shape[0] == 1, "single-layer block: the final rmsnorm is fused into the channel-mix call"
    cos_t, sin_t = _rope_tables(seq)
    mod = _modulation(c, ada_w, ada_b).reshape(batch, N_MOD, d)
    w_proj = _pack_rows(jnp.swapaxes(w_in, 1, 2), PROJ_W, transposed=True)
    mixed = _token_mix(
        x, mod, norm_mix_w[0].reshape(1, d), w_proj, cos_t, sin_t, conv_w[0],
        _pad_lanes(a_log[0], GAB_W), _pad_lanes(dt_bias[0], GAB_W),
        ret_norm_w[0].reshape(1, GROUP_W), gdn_norm_w[0].reshape(1, HEAD_DIM))
    return _channel_mix(x, mixed, mod, _pack_rows(w_out), norm_mlp_w[0].reshape(1, d),
                        _pack_rows(w_ff1), _pack_rows(w_ff2), norm_final_w.reshape(1, d))
```

```python
import math

import jax
import jax.numpy as jnp
from jax import lax
from jax.experimental import pallas as pl
from jax.experimental.pallas import tpu as pltpu

CHUNK = 64
HEADS = 4
HEAD_DIM = 128
GROUP_W = HEADS * HEAD_DIM
CONV_WIDTH = 4
CONV_CH = 3 * GROUP_W
CONV_PAD = 8
ROPE_BASE = 10000.0
NORM_EPS = 1e-6
N_MOD = 6

TIME_BLOCK = 256
PROJ_TILE = 256
ROW_BLOCK = 1024
FF_BLOCK = 1024
ROPE_BLOCK = 512
LANES = 128
PACK_BLOCK_BYTES = 6 * 1024 * 1024
MIX_VMEM_LIMIT_BYTES = 48 * 1024 * 1024
CHANNEL_VMEM_LIMIT_BYTES = 56 * 1024 * 1024

COL_RET = 0
COL_CONV = 4 * GROUP_W
COL_GZ = COL_CONV + CONV_CH
COL_GAB = COL_GZ + GROUP_W
GAB_W = LANES
PROJ_W = COL_GAB + GAB_W

NT_DIMS = (((1,), (1,)), ((), ()))
TN_DIMS = (((0,), (0,)), ((), ()))

BF16 = jnp.bfloat16
F32 = jnp.float32


def _dot(a, b):
    return jnp.dot(a.astype(BF16), b.astype(BF16), preferred_element_type=F32)


def _dot_nt(a, b):
    return lax.dot_general(a.astype(BF16), b.astype(BF16), NT_DIMS, preferred_element_type=F32)


def _dot_tn(a, b):
    return lax.dot_general(a.astype(BF16), b.astype(BF16), TN_DIMS, preferred_element_type=F32)


def _pack_kernel(n_valid, col_axis, w_ref, o_ref):
    w = w_ref[...]
    tile = w.shape[col_axis]
    if n_valid % tile:
        col = pl.program_id(0) * tile + lax.broadcasted_iota(jnp.int32, w.shape, col_axis)
        w = jnp.where(col < n_valid, w, 0.0)
    if col_axis == 0:
        w = w.T
    o_ref[...] = pltpu.bitcast(w.astype(BF16), jnp.uint32)


def _pack_rows(w, n_out=None, transposed=False):
    k, n = (w.shape[2], w.shape[1]) if transposed else (w.shape[1], w.shape[2])
    n_out = n if n_out is None else n_out
    tile = max(t for t in range(LANES, n_out + 1, LANES)
               if n_out % t == 0 and (k * t * 4 <= PACK_BLOCK_BYTES or t == LANES))
    in_spec = (pl.BlockSpec((None, tile, k), lambda j: (0, j, 0)) if transposed
               else pl.BlockSpec((None, k, tile), lambda j: (0, 0, j)))
    return pl.pallas_call(
        lambda w_ref, o_ref: _pack_kernel(n, 0 if transposed else 1, w_ref, o_ref),
        grid=(n_out // tile,),
        in_specs=[in_spec],
        out_specs=pl.BlockSpec((k // 2, tile), lambda j: (0, j)),
        out_shape=jax.ShapeDtypeStruct((k // 2, n_out), jnp.uint32),
        name="pack_weight",
    )(w)


def _unpack_rows(w):
    return pltpu.bitcast(w, BF16)


def _sigmoid(x):
    return 1.0 / (1.0 + jnp.exp(-x))


def _silu(x):
    return x * _sigmoid(x)


def _softplus(x):
    return jnp.maximum(x, 0.0) + jnp.log(1.0 + jnp.exp(-jnp.abs(x)))


def _rms(x):
    return x * lax.rsqrt(jnp.mean(x * x, axis=-1, keepdims=True) + NORM_EPS)


def _mod_kernel(c_ref, w_ref, b_ref, o_ref):
    o_ref[...] = _dot(_silu(c_ref[...]), w_ref[...]) + b_ref[...]


def _modulation(c, ada_w, ada_b):
    batch, d = c.shape
    n = ada_w.shape[2]
    return pl.pallas_call(
        _mod_kernel,
        grid=(n // d,),
        in_specs=[
            pl.BlockSpec((batch, d), lambda j: (0, 0)),
            pl.BlockSpec((None, d, d), lambda j: (0, 0, j)),
            pl.BlockSpec((1, d), lambda j: (0, j)),
        ],
        out_specs=pl.BlockSpec((batch, d), lambda j: (0, j)),
        out_shape=jax.ShapeDtypeStruct((batch, n), F32),
        name="modulation",
    )(c, ada_w, ada_b)


def _rope_kernel(cos_ref, sin_ref):
    rows = cos_ref.shape[0]
    pos = (pl.program_id(0) * rows + lax.broadcasted_iota(jnp.int32, (rows, HEAD_DIM), 0)).astype(F32)
    lane = lax.broadcasted_iota(jnp.int32, (rows, HEAD_DIM), 1)
    half = HEAD_DIM // 2
    freq = jnp.where(lane < half, lane, lane - half).astype(F32)
    inv = jnp.exp(freq * (-2.0 * math.log(ROPE_BASE) / HEAD_DIM))
    ang = pos * inv
    cos_ref[...] = jnp.cos(ang)
    sin_ref[...] = jnp.where(lane < half, -jnp.sin(ang), jnp.sin(ang))


def _rope_tables(seq):
    spec = pl.BlockSpec((ROPE_BLOCK, HEAD_DIM), lambda i: (i, 0))
    shape = jax.ShapeDtypeStruct((seq, HEAD_DIM), F32)
    return pl.pallas_call(
        _rope_kernel, grid=(seq // ROPE_BLOCK,), in_specs=[], out_specs=[spec, spec],
        out_shape=[shape, shape], name="rope_tables")()


def _mix_kernel(x_ref, mod_ref, nw_ref, w_ref, cos_ref, sin_ref, cw_ref, alog_ref, dtb_ref,
                rnw_ref, gnw_ref, o_ref,
                ret_a, xpad_a, gz_a, gab_a, ret_b, xpad_b, gz_b, gab_b, rstate_ref, gstate_ref,
                dmask_ref, qdec_ref, kdec_ref):
    tb = ret_a.shape[0]
    n_chunks = tb // CHUNK
    log_gamma = [math.log(1.0 - 2.0 ** (-5.0 - h)) for h in range(HEADS)]
    key_scale = HEAD_DIM ** -0.5
    heads = range(HEADS)
    blocks = (((ret_a, xpad_a, gz_a, gab_a), 0), ((ret_b, xpad_b, gz_b, gab_b), 1))

    def head_cols(group, h):
        return slice(group * GROUP_W + h * HEAD_DIM, group * GROUP_W + (h + 1) * HEAD_DIM)

    @pl.when((pl.program_id(0) == 0) & (pl.program_id(1) == 0))
    def _():
        row = lax.broadcasted_iota(jnp.int32, (tb, tb), 0)
        col = lax.broadcasted_iota(jnp.int32, (tb, tb), 1)
        dist = jnp.abs(row - col).astype(F32)
        visible = (col // CHUNK) <= (row // CHUNK)
        ridx = lax.broadcasted_iota(jnp.int32, (tb, HEAD_DIM), 0).astype(F32)
        for h in range(HEADS):
            dmask_ref[h] = jnp.where(visible, jnp.exp(log_gamma[h] * dist) * key_scale, 0.0)
            qdec_ref[h] = jnp.exp(log_gamma[h] * (ridx + 1.0))
            kdec_ref[h] = jnp.exp(log_gamma[h] * (tb - 1.0 - ridx)) * key_scale

    @pl.when(pl.program_id(1) == 0)
    def _():
        rstate_ref[...] = jnp.zeros_like(rstate_ref)
        gstate_ref[...] = jnp.zeros_like(gstate_ref)
        xpad_a[0:CONV_PAD, :] = jnp.zeros((CONV_PAD, CONV_CH), F32)
        xpad_b[0:CONV_PAD, :] = jnp.zeros((CONV_PAD, CONV_CH), F32)

    def project_block(bufs, blk):
        ret_ref, xpad_ref, gz_ref, gab_ref = bufs
        shift = mod_ref[blk, 0:1, :]
        scale = mod_ref[blk, 1:2, :]
        h_in = (_rms(x_ref[blk]) * (nw_ref[...] * (1.0 + scale)) + shift).astype(BF16)

        def project(dest_ref, row0, col0, width):
            for c in range(0, width, PROJ_TILE):
                n = min(PROJ_TILE, width - c)
                w_tile = _unpack_rows(w_ref[:, col0 + c:col0 + c + n])
                dest_ref[row0:row0 + tb, c:c + n] = jnp.dot(h_in, w_tile, preferred_element_type=F32)

        project(gab_ref, 0, COL_GAB, GAB_W)
        project(xpad_ref, CONV_PAD, COL_CONV, CONV_CH)
        project(ret_ref, 0, COL_RET, COL_CONV - COL_RET)
        project(gz_ref, 0, COL_GZ, GROUP_W)

    def delta_front(bufs):
        _, xpad_ref, _, gab_ref = bufs
        xp = xpad_ref[...]
        conv = xp * cw_ref[0:1, :]
        for w in range(1, CONV_WIDTH):
            conv = xp * cw_ref[w:w + 1, :] + pltpu.roll(conv, 1, 0)
        qkv = _silu(conv[CONV_PAD:, :])

        gab = gab_ref[...]
        g = -jnp.exp(alog_ref[...]) * _softplus(gab + dtb_ref[...])
        beta = _sigmoid(gab)
        in_chunk = lax.broadcasted_iota(jnp.int32, (tb, GAB_W), 0) % CHUNK
        gc = g
        step = 1
        while step < CHUNK:
            gc = gc + jnp.where(in_chunk >= step, pltpu.roll(gc, step, 0), 0.0)
            step *= 2

        f = dict(gq=[], gk=[], gk_b=[], gv=[], gch=[], bh=[], e_gc=[], k_dec=[])
        for h in heads:
            qh = qkv[:, head_cols(0, h)]
            kh = qkv[:, head_cols(1, h)]
            f["gq"].append(qh * lax.rsqrt(jnp.sum(qh * qh, axis=-1, keepdims=True) + NORM_EPS) * (HEAD_DIM ** -0.5))
            gk = kh * lax.rsqrt(jnp.sum(kh * kh, axis=-1, keepdims=True) + NORM_EPS)
            f["gk"].append(gk)
            f["gk_b"].append(gk.astype(BF16))
            f["gv"].append(qkv[:, head_cols(2, h)])
            gch = jnp.broadcast_to(gc[:, h:h + 1], (tb, HEAD_DIM))
            f["gch"].append(gch)
            f["bh"].append(jnp.broadcast_to(beta[:, HEADS + h:HEADS + h + 1], (tb, HEAD_DIM)))
            f["e_gc"].append(jnp.exp(gch))
            g_last = jnp.concatenate(
                [jnp.broadcast_to(gch[(c + 1) * CHUNK - 1:(c + 1) * CHUNK, :], (CHUNK, HEAD_DIM))
                 for c in range(n_chunks)], axis=0)
            f["k_dec"].append(gk * jnp.exp(g_last - gch))
        return f

    def retention(bufs, blk):
        ret_ref = bufs[0]
        cos = cos_ref[...]
        sin = sin_ref[...]
        rq, rk, rvb = [], [], []
        for h in heads:
            q = ret_ref[:, head_cols(0, h)]
            k = ret_ref[:, head_cols(1, h)]
            rq.append(q * cos + pltpu.roll(q, HEAD_DIM // 2, 1) * sin)
            rk.append(k * cos + pltpu.roll(k, HEAD_DIM // 2, 1) * sin)
            rvb.append(ret_ref[:, head_cols(2, h)].astype(BF16))
        scores = [_dot_nt(rq[h], rk[h]) * dmask_ref[h] for h in heads]
        rstates = [rstate_ref[blk, h] for h in heads]
        ro = [_dot(scores[h], rvb[h]) + _dot(rq[h] * qdec_ref[h], rstates[h]) for h in heads]
        for h in heads:
            rstate_ref[blk, h] = rstates[h] * math.exp(log_gamma[h] * tb) + _dot_tn(rk[h] * kdec_ref[h], rvb[h])
        for h in heads:
            mu = jnp.mean(ro[h], axis=-1, keepdims=True)
            d = ro[h] - mu
            var = jnp.mean(d * d, axis=-1, keepdims=True)
            y = d * lax.rsqrt(var + NORM_EPS) * rnw_ref[:, head_cols(0, h)] * _silu(ret_ref[:, head_cols(3, h)])
            o_ref[blk, :, head_cols(0, h)] = y.astype(o_ref.dtype)

    for bufs, blk in blocks:
        project_block(bufs, blk)
    fronts = [delta_front(bufs) for bufs, _ in blocks]
    for bufs, blk in blocks:
        retention(bufs, blk)
    for (_, xpad_ref, _, _), _ in blocks:
        xpad_ref[0:CONV_PAD, :] = xpad_ref[tb:tb + CONV_PAD, :]

    pair = 2 * CHUNK
    units = [(i, h, p) for i in range(len(blocks)) for p in range(tb // pair) for h in heads]
    pcol = lax.broadcasted_iota(jnp.int32, (pair, pair), 1)
    frow = lax.broadcasted_iota(jnp.int32, (CHUNK, pair), 0)
    flane = lax.broadcasted_iota(jnp.int32, (CHUNK, pair), 1)
    first = flane < CHUNK
    causal = frow >= flane % CHUNK
    diag = frow == flane % CHUNK
    eye = jnp.where(diag, 1.0, 0.0)

    def fold(a):
        return jnp.where(first, a[:CHUNK], a[CHUNK:])

    def chunk_lanes(a, c):
        return jnp.where(first if c == 0 else ~first, a, jnp.zeros_like(a))

    def unfold(a):
        return jnp.concatenate([chunk_lanes(a, 0), chunk_lanes(a, 1)], axis=0)

    def block_diag_dot(a, b):
        return jnp.concatenate([_dot(chunk_lanes(a, 0), b), _dot(chunk_lanes(a, 1), b)], axis=0)

    def unit_rows(p):
        return slice(p * pair, (p + 1) * pair)

    def operand(name, i, h, p):
        return fronts[i][name][h][unit_rows(p)]

    kk = [fold(_dot_nt(operand("gk_b", *u), operand("gk_b", *u))) for u in units]
    qk = [fold(_dot_nt(operand("gq", *u), operand("gk_b", *u))) for u in units]
    decay, power, inv = [], [], []
    for n, u in enumerate(units):
        gm = operand("gch", *u)
        decay.append(jnp.exp(jnp.where(causal, fold(gm) - gm.T[:CHUNK], -jnp.inf)))
        power.append(jnp.where(diag, 0.0, -(fold(operand("bh", *u)) * kk[n] * decay[n])))
        inv.append(eye + power[n])
    level = 2
    while level < CHUNK:
        power = [_dot(pw, unfold(pw.astype(BF16))) for pw in power]
        inv = [iv + _dot(iv, unfold(pw.astype(BF16))) for iv, pw in zip(inv, power)]
        level *= 2
    uw = []
    for n, u in enumerate(units):
        b = operand("bh", *u)
        rhs = jnp.concatenate([b * operand("gv", *u), b * operand("gk", *u) * operand("e_gc", *u)], axis=1)
        uw.append(block_diag_dot(inv[n].astype(BF16), rhs.astype(BF16)))
    a_uw = [block_diag_dot((qk[n] * decay[n]).astype(BF16), uw[n].astype(BF16))
            for n in range(len(units))]
    q_eff, k_uw = [], []
    for n, u in enumerate(units):
        q_eff.append(operand("gq", *u) * operand("e_gc", *u) - a_uw[n][:, HEAD_DIM:])
        kd_t = operand("k_dec", *u).T
        k_uw.append([_dot(jnp.where(pcol // CHUNK == c, kd_t, 0.0), uw[n])
                     for c in range(pair // CHUNK)])
    gstates = [[gstate_ref[i, h] for h in heads] for i in range(len(blocks))]
    outs = [[[] for _ in heads] for _ in blocks]
    for c in range(n_chunks):
        p, cl = divmod(c, pair // CHUNK)
        local = slice(cl * CHUNK, (cl + 1) * CHUNK)
        for i in range(len(blocks)):
            for h in heads:
                n = units.index((i, h, p))
                sb = gstates[i][h].astype(BF16)
                outs[i][h].append(_dot(q_eff[n][local], sb) + a_uw[n][local, :HEAD_DIM])
                gch = fronts[i]["gch"][h]
                chunk_decay = jnp.exp(gch[(c + 1) * CHUNK - 1:(c + 1) * CHUNK, :])
                gstates[i][h] = (gstates[i][h] * chunk_decay - _dot(k_uw[n][cl][:, HEAD_DIM:], sb)
                                 + k_uw[n][cl][:, :HEAD_DIM])
    for i, (bufs, blk) in enumerate(blocks):
        gz_ref = bufs[2]
        for h in heads:
            gstate_ref[i, h] = gstates[i][h]
            o = jnp.concatenate(outs[i][h], axis=0)
            y = _rms(o) * gnw_ref[...] * _silu(gz_ref[:, head_cols(0, h)])
            o_ref[blk, :, head_cols(1, h)] = y.astype(o_ref.dtype)


def _token_mix(x, mod, norm_w, w_proj, cos_t, sin_t, conv_w, a_log_pad, dt_pad, ret_norm_w, gdn_norm_w):
    batch, seq, d = x.shape
    tb = TIME_BLOCK
    pair_of_seqs = 2
    assert batch % pair_of_seqs == 0 and seq % tb == 0 and d % 2 == 0 and w_proj.shape == (d // 2, PROJ_W)
    const = lambda shape: pl.BlockSpec(shape, lambda b, t: (0,) * len(shape))
    proj_bufs = [
        pltpu.VMEM((tb, 4 * GROUP_W), F32),
        pltpu.VMEM((tb + CONV_PAD, CONV_CH), F32),
        pltpu.VMEM((tb, GROUP_W), F32),
        pltpu.VMEM((tb, GAB_W), F32),
    ]
    return pl.pallas_call(
        _mix_kernel,
        grid=(batch // pair_of_seqs, seq // tb),
        in_specs=[
            pl.BlockSpec((pair_of_seqs, tb, d), lambda b, t: (b, t, 0)),
            pl.BlockSpec((pair_of_seqs, N_MOD, d), lambda b, t: (b, 0, 0)),
            const((1, d)),
            pl.BlockSpec((d // 2, PROJ_W), lambda b, t: (0, 0), pipeline_mode=pl.Buffered(1)),
            pl.BlockSpec((tb, HEAD_DIM), lambda b, t: (t, 0)),
            pl.BlockSpec((tb, HEAD_DIM), lambda b, t: (t, 0)),
            const((CONV_WIDTH, CONV_CH)),
            const((1, GAB_W)),
            const((1, GAB_W)),
            const((1, GROUP_W)),
            const((1, HEAD_DIM)),
        ],
        out_specs=pl.BlockSpec((pair_of_seqs, tb, 2 * GROUP_W), lambda b, t: (b, t, 0)),
        out_shape=jax.ShapeDtypeStruct((batch, seq, 2 * GROUP_W), BF16),
        scratch_shapes=proj_bufs + proj_bufs + [
            pltpu.VMEM((pair_of_seqs, HEADS, HEAD_DIM, HEAD_DIM), F32),
            pltpu.VMEM((pair_of_seqs, HEADS, HEAD_DIM, HEAD_DIM), F32),
            pltpu.VMEM((HEADS, tb, tb), F32),
            pltpu.VMEM((HEADS, tb, HEAD_DIM), F32),
            pltpu.VMEM((HEADS, tb, HEAD_DIM), F32),
        ],
        compiler_params=pltpu.CompilerParams(
            dimension_semantics=("arbitrary", "arbitrary"), vmem_limit_bytes=MIX_VMEM_LIMIT_BYTES),
        name="token_mix",
    )(x, mod, norm_w, w_proj, cos_t, sin_t, conv_w, a_log_pad, dt_pad, ret_norm_w, gdn_norm_w)


def _channel_kernel(x_ref, mixed_ref, mod_ref, wo_ref, nw_ref, w1_ref, w2_ref, fw_ref, o_ref):
    gate_a = mod_ref[2:3, :]
    shift = mod_ref[3:4, :]
    scale = mod_ref[4:5, :]
    gate_m = mod_ref[5:6, :]
    x1 = x_ref[...] + gate_a * jnp.dot(mixed_ref[...], _unpack_rows(wo_ref[...]), preferred_element_type=F32)
    h = (_rms(x1) * nw_ref[...] * (1.0 + scale) + shift).astype(BF16)
    d_ff = w1_ref.shape[1]
    acc = jnp.zeros(x1.shape, F32)
    for j in range(d_ff // FF_BLOCK):
        cols = slice(j * FF_BLOCK, (j + 1) * FF_BLOCK)
        a = jnp.maximum(jnp.dot(h, _unpack_rows(w1_ref[:, cols]), preferred_element_type=F32), 0.0)
        w2_rows = _unpack_rows(w2_ref[j * FF_BLOCK // 2:(j + 1) * FF_BLOCK // 2, :])
        acc = acc + jnp.dot((a * a).astype(BF16), w2_rows, preferred_element_type=F32)
    x2 = x1 + gate_m * acc
    o_ref[...] = _rms(x2) * fw_ref[...]


def _channel_mix(x, mixed, mod, w_out, norm_w, w_ff1, w_ff2, final_w):
    batch, seq, d = x.shape
    d_ff = w_ff1.shape[1]
    rb = ROW_BLOCK
    assert seq % rb == 0 and d_ff % FF_BLOCK == 0
    per_batch = seq // rb
    resident = lambda shape: pl.BlockSpec(shape, lambda i: (0, 0), pipeline_mode=pl.Buffered(1))
    out = pl.pallas_call(
        _channel_kernel,
        grid=(batch * per_batch,),
        in_specs=[
            pl.BlockSpec((rb, d), lambda i: (i, 0)),
            pl.BlockSpec((rb, mixed.shape[-1]), lambda i: (i, 0)),
            pl.BlockSpec((None, N_MOD, d), lambda i: (i // per_batch, 0, 0)),
            resident((mixed.shape[-1] // 2, d)),
            pl.BlockSpec((1, d), lambda i: (0, 0)),
            resident((d // 2, d_ff)),
            resident((d_ff // 2, d)),
            pl.BlockSpec((1, d), lambda i: (0, 0)),
        ],
        out_specs=pl.BlockSpec((rb, d), lambda i: (i, 0)),
        out_shape=jax.ShapeDtypeStruct((batch * seq, d), F32),
        compiler_params=pltpu.CompilerParams(
            dimension_semantics=("arbitrary",), vmem_limit_bytes=CHANNEL_VMEM_LIMIT_BYTES),
        name="channel_mix",
    )(x.reshape(batch * seq, d), mixed.reshape(batch * seq, -1), mod, w_out, norm_w, w_ff1, w_ff2, final_w)
    return out.reshape(batch, seq, d)


def _pad_lanes(v, width):
    return jnp.pad(v, (0, width - v.shape[0])).reshape(1, width)


def kernel(x, c, ada_w, ada_b, norm_mix_w, w_in, conv_w, a_log, dt_bias, ret_norm_w, gdn_norm_w, w_out,
           norm_mlp_w, w_ff1, w_ff2, norm_final_w):
    batch, seq, d = x.shape
    assert ada_w.shape[0] == 1, "single-layer block: the final rmsnorm is fused into the channel-mix call"
    cos_t, sin_t = _rope_tables(seq)
    mod = _modulation(c, ada_w, ada_b).reshape(batch, N_MOD, d)
    w_proj = _pack_rows(jnp.swapaxes(w_in, 1, 2), PROJ_W, transposed=True)
    mixed = _token_mix(
        x, mod, norm_mix_w[0].reshape(1, d), w_proj, cos_t, sin_t, conv_w[0],
        _pad_lanes(a_log[0], GAB_W), _pad_lanes(dt_bias[0], GAB_W),
        ret_norm_w[0].reshape(1, GROUP_W), gdn_norm_w[0].reshape(1, HEAD_DIM))
    return _channel_mix(x, mixed, mod, _pack_rows(w_out), norm_mlp_w[0].reshape(1, d),
                        _pack_rows(w_ff1), _pack_rows(w_ff2), norm_final_w.reshape(1, d))
```
